```python
import jax, jax.numpy as jnp
from jax import lax
import numpy as np

D_MODEL = 1024
BATCH = 16
SEQ = 2048
DEPTH = 1

N_MEM = 256
D_LRU = D_MODEL // 2
LRU_BLOCKS = 8
LRU_BLOCK_DIM = D_LRU // LRU_BLOCKS
CONV_WIDTH = 4
LRU_C = 8.0
D_ATTN = D_MODEL - D_LRU
ATTN_HEADS = 8
ATTN_HEAD_DIM = D_ATTN // ATTN_HEADS
MOBA_BLOCK = 256
MOBA_TOPK = 3
MOBA_Q_CHUNK = 16
D_IN = 2 * D_LRU + 3 * D_ATTN
CROSS_HEADS = 4
CROSS_HEAD_DIM = D_MODEL // CROSS_HEADS
PEER_KEYS = 128
PEER_EXPERTS = PEER_KEYS * PEER_KEYS
PEER_HEADS = 8
PEER_TOPK = 16
PEER_QDIM = 256
PEER_HALF = PEER_QDIM // 2
PEER_TOKEN_CHUNK = 128
PEER_V_SCALE = 0.5

EPS = 1e-6
NEG = -1e30

kernel_name = "hymba_rglru_moba_peer_block"


def rmsnorm(x, g):
    xf = x.astype(jnp.float32)
    y = xf * lax.rsqrt(jnp.mean(xf * xf, axis=-1, keepdims=True) + EPS)
    return (y * g.astype(jnp.float32)).astype(x.dtype)


def causal_depthwise_conv(x, w, b):
    y = lax.conv_general_dilated(
        x, w[:, None, :], window_strides=(1,), padding=[(CONV_WIDTH - 1, 0)],
        dimension_numbers=('NWC', 'WIO', 'NWC'), feature_group_count=x.shape[-1])
    return y + b


def rg_lru(x, wa, ba, wx, bx, lam):
    B, S, C = x.shape
    xb = x.reshape(B, S, LRU_BLOCKS, LRU_BLOCK_DIM)
    gate_r = jax.nn.sigmoid(jnp.einsum('bsnd,nde->bsne', xb, wa).reshape(B, S, C) + ba)
    gate_i = jax.nn.sigmoid(jnp.einsum('bsnd,nde->bsne', xb, wx).reshape(B, S, C) + bx)
    log_a = (-LRU_C * gate_r.astype(jnp.float32)) * jax.nn.softplus(-lam.astype(jnp.float32))
    a = jnp.exp(log_a)
    b = jnp.sqrt(-jnp.expm1(2.0 * log_a)) * (gate_i * x).astype(jnp.float32)

    def combine(left, right):
        a1, b1 = left
        a2, b2 = right
        return a1 * a2, a2 * b1 + b2

    _, h = lax.associative_scan(combine, (a, b), axis=1)
    return h.astype(x.dtype)


def moba_attention(q, k, v):
    B, S, H, Dh = q.shape
    nb = -(-S // MOBA_BLOCK)
    s_pad = nb * MOBA_BLOCK
    pad = [(0, 0), (0, s_pad - S), (0, 0), (0, 0)]
    q, k, v = (jnp.pad(t, pad).transpose(0, 2, 1, 3) for t in (q, k, v))
    scale = Dh ** -0.5
    k_blocks = k.reshape(B, H, nb, MOBA_BLOCK, Dh)
    v_blocks = v.reshape(B, H, nb, MOBA_BLOCK, Dh)
    k_mean = jnp.mean(k_blocks.astype(jnp.float32), axis=3)
    q_block = jnp.arange(s_pad) // MOBA_BLOCK
    gate = jnp.einsum('bhsd,bhnd->bhsn', q.astype(jnp.float32), k_mean)
    fully_past = jnp.arange(nb)[None, :] < q_block[:, None]
    gate = jnp.where(fully_past, gate, NEG)
    topk = min(MOBA_TOPK, nb)
    _, sel = lax.top_k(gate, topk)
    sel_valid = sel < q_block[:, None]
    gather_blocks = jax.vmap(jax.vmap(lambda blocks, idx: blocks[idx]))
    n_chunks = s_pad // MOBA_Q_CHUNK

    def chunk(c):
        start = c * MOBA_Q_CHUNK
        q_c = lax.dynamic_slice_in_dim(q, start, MOBA_Q_CHUNK, axis=2)
        sel_c = lax.dynamic_slice_in_dim(sel, start, MOBA_Q_CHUNK, axis=2)
        valid_c = lax.dynamic_slice_in_dim(sel_valid, start, MOBA_Q_CHUNK, axis=2)
        flat_sel = sel_c.reshape(B, H, MOBA_Q_CHUNK * topk)
        k_sel = gather_blocks(k_blocks, flat_sel).reshape(B, H, MOBA_Q_CHUNK, topk * MOBA_BLOCK, Dh)
        v_sel = gather_blocks(v_blocks, flat_sel).reshape(B, H, MOBA_Q_CHUNK, topk * MOBA_BLOCK, Dh)
        j = start // MOBA_BLOCK
        k_own = lax.dynamic_index_in_dim(k_blocks, j, axis=2, keepdims=False)
        v_own = lax.dynamic_index_in_dim(v_blocks, j, axis=2, keepdims=False)
        s_past = jnp.einsum('bhqd,bhqkd->bhqk', q_c, k_sel).astype(jnp.float32) * scale
        s_past = jnp.where(jnp.repeat(valid_c, MOBA_BLOCK, axis=-1), s_past, NEG)
        s_own = jnp.einsum('bhqd,bhkd->bhqk', q_c, k_own).astype(jnp.float32) * scale
        q_pos = start + jnp.arange(MOBA_Q_CHUNK)
        k_pos = j * MOBA_BLOCK + jnp.arange(MOBA_BLOCK)
        s_own = jnp.where(k_pos[None, :] <= q_pos[:, None], s_own, NEG)
        p = jax.nn.softmax(jnp.concatenate([s_past, s_own], axis=-1), axis=-1).astype(v.dtype)
        p_past, p_own = p[..., :topk * MOBA_BLOCK], p[..., topk * MOBA_BLOCK:]
        return (jnp.einsum('bhqk,bhqkd->bhqd', p_past, v_sel)
                + jnp.einsum('bhqk,bhkd->bhqd', p_own, v_own))

    out = lax.map(chunk, jnp.arange(n_chunks))
    out = out.transpose(1, 0, 3, 2, 4).reshape(B, s_pad, H, Dh)
    return out[:, :S]


def memory_cross_attention(xn, memn, w_q, w_k, w_v, w_o):
    B, S, _ = xn.shape
    M = memn.shape[1]
    q = (xn @ w_q).reshape(B, S, CROSS_HEADS, CROSS_HEAD_DIM)
    k = (memn @ w_k).reshape(B, M, CROSS_HEADS, CROSS_HEAD_DIM)
    v = (memn @ w_v).reshape(B, M, CROSS_HEADS, CROSS_HEAD_DIM)
    s = jnp.einsum('bshd,bmhd->bhsm', q, k).astype(jnp.float32) * CROSS_HEAD_DIM ** -0.5
    p = jax.nn.softmax(s, axis=-1).astype(v.dtype)
    o = jnp.einsum('bhsm,bmhd->bshd', p, v).reshape(B, S, D_MODEL)
    return o @ w_o


def peer_ffn(xn, w_q, sub_keys, u, v):
    B, S, D = xn.shape
    T = B * S
    t = xn.reshape(T, D)
    q = (t @ w_q).reshape(T, PEER_HEADS, 2, PEER_HALF)
    s = jnp.einsum('thpd,hpnd->thpn', q, sub_keys).astype(jnp.float32)
    s_top, i_top = lax.top_k(s, PEER_TOPK)
    cand_s = (s_top[:, :, 0, :, None] + s_top[:, :, 1, None, :]).reshape(T, PEER_HEADS, PEER_TOPK * PEER_TOPK)
    cand_i = (i_top[:, :, 0, :, None] * PEER_KEYS + i_top[:, :, 1, None, :]).reshape(T, PEER_HEADS, PEER_TOPK * PEER_TOPK)
    best_s, pos = lax.top_k(cand_s, PEER_TOPK)
    ids = jnp.take_along_axis(cand_i, pos, axis=-1).reshape(T, PEER_HEADS * PEER_TOPK)
    gates = jax.nn.softmax(best_s, axis=-1).reshape(T, PEER_HEADS * PEER_TOPK)
    n_chunks = T // PEER_TOKEN_CHUNK

    def chunk(args):
        x_c, ids_c, g_c = args
        act = jax.nn.gelu(jnp.einsum('td,ted->te', x_c, u[ids_c]), approximate=False)
        return jnp.einsum('te,ted->td', g_c.astype(x_c.dtype) * act, v[ids_c])

    y = lax.map(chunk, (t.reshape(n_chunks, PEER_TOKEN_CHUNK, D),
                        ids.reshape(n_chunks, PEER_TOKEN_CHUNK, -1),
                        gates.reshape(n_chunks, PEER_TOKEN_CHUNK, -1)))
    return y.reshape(B, S, D)


def hybrid_layer(x, mem, ln_mix_g, w_in, conv_w, conv_b, lru_wa, lru_ba, lru_wx, lru_bx,
                 lru_lambda, gn_lru_g, gn_attn_g, w_out, ln_cross_g, ln_mem_g, w_cq, w_ck,
                 w_cv, w_co, ln_ffn_g, peer_wq, peer_subkeys, peer_u, peer_v):
    B, S, _ = x.shape
    h = rmsnorm(x, ln_mix_g)
    proj = h @ w_in
    lru_x, lru_gate, q, k, v = jnp.split(
        proj, [D_LRU, 2 * D_LRU, 2 * D_LRU + D_ATTN, 2 * D_LRU + 2 * D_ATTN], axis=-1)
    lru_x = causal_depthwise_conv(lru_x, conv_w, conv_b)
    y_lru = rg_lru(lru_x, lru_wa, lru_ba, lru_wx, lru_bx, lru_lambda) * jax.nn.gelu(lru_gate, approximate=False)
    shp = (B, S, ATTN_HEADS, ATTN_HEAD_DIM)
    y_attn = moba_attention(q.reshape(shp), k.reshape(shp), v.reshape(shp)).reshape(B, S, D_ATTN)
    mixed = jnp.concatenate([rmsnorm(y_lru, gn_lru_g), rmsnorm(y_attn, gn_attn_g)], axis=-1)
    x = x + mixed @ w_out
    x = x + memory_cross_attention(rmsnorm(x, ln_cross_g), rmsnorm(mem, ln_mem_g), w_cq, w_ck, w_cv, w_co)
    x = x + peer_ffn(rmsnorm(x, ln_ffn_g), peer_wq, peer_subkeys, peer_u, peer_v)
    return x


def setup_inputs(seed: int = 0) -> dict:
    key = jax.random.key(seed)
    ks = jax.random.split(key, 32)

    def normal(k, shape, scale):
        return jax.random.normal(k, shape, jnp.float32) * scale

    def gain(k, shape):
        return 1.0 + normal(k, shape, 0.02)

    L = DEPTH
    u_a = jax.random.uniform(ks[8], (L, D_LRU), jnp.float32, minval=0.9, maxval=0.999)
    p_a = u_a ** (1.0 / LRU_C)
    lru_lambda = jnp.log(p_a) - jnp.log1p(-p_a)
    return {
        "x": normal(ks[0], (BATCH, SEQ, D_MODEL), 1.0),
        "mem": normal(ks[1], (BATCH, N_MEM, D_MODEL), 1.0),
        "ln_mix_g": gain(ks[2], (L, D_MODEL)),
        "w_in": normal(ks[3], (L, D_MODEL, D_IN), D_MODEL ** -0.5),
        "conv_w": normal(ks[4], (L, CONV_WIDTH, D_LRU), CONV_WIDTH ** -0.5),
        "conv_b": normal(ks[5], (L, D_LRU), 0.01),
        "lru_wa": normal(ks[6], (L, LRU_BLOCKS, LRU_BLOCK_DIM, LRU_BLOCK_DIM), LRU_BLOCK_DIM ** -0.5),
        "lru_ba": normal(ks[7], (L, D_LRU), 0.01),
        "lru_wx": normal(ks[9], (L, LRU_BLOCKS, LRU_BLOCK_DIM, LRU_BLOCK_DIM), LRU_BLOCK_DIM ** -0.5),
        "lru_bx": normal(ks[10], (L, D_LRU), 0.01),
        "lru_lambda": lru_lambda,
        "gn_lru_g": gain(ks[11], (L, D_LRU)),
        "gn_attn_g": gain(ks[12], (L, D_ATTN)),
        "w_out": normal(ks[13], (L, D_MODEL, D_MODEL), D_MODEL ** -0.5),
        "ln_cross_g": gain(ks[14], (L, D_MODEL)),
        "ln_mem_g": gain(ks[15], (L, D_MODEL)),
        "w_cq": normal(ks[16], (L, D_MODEL, D_MODEL), D_MODEL ** -0.5),
        "w_ck": normal(ks[17], (L, D_MODEL, D_MODEL), D_MODEL ** -0.5),
        "w_cv": normal(ks[18], (L, D_MODEL, D_MODEL), D_MODEL ** -0.5),
        "w_co": normal(ks[19], (L, D_MODEL, D_MODEL), D_MODEL ** -0.5),
        "ln_ffn_g": gain(ks[20], (L, D_MODEL)),
        "peer_wq": normal(ks[21], (L, D_MODEL, PEER_HEADS * PEER_QDIM), D_MODEL ** -0.5),
        "peer_subkeys": normal(ks[22], (L, PEER_HEADS, 2, PEER_KEYS, PEER_HALF), PEER_HALF ** -0.5),
        "peer_u": normal(ks[23], (L, PEER_EXPERTS, D_MODEL), D_MODEL ** -0.5),
        "peer_v": normal(ks[24], (L, PEER_EXPERTS, D_MODEL), PEER_V_SCALE),
        "ln_final_g": gain(ks[25], (D_MODEL,)),
    }


def reference(x, mem, ln_mix_g, w_in, conv_w, conv_b, lru_wa, lru_ba, lru_wx, lru_bx,
              lru_lambda, gn_lru_g, gn_attn_g, w_out, ln_cross_g, ln_mem_g, w_cq, w_ck,
              w_cv, w_co, ln_ffn_g, peer_wq, peer_subkeys, peer_u, peer_v, ln_final_g):
    for l in range(DEPTH):
        x = hybrid_layer(x, mem, ln_mix_g[l], w_in[l], conv_w[l], conv_b[l], lru_wa[l], lru_ba[l],
                         lru_wx[l], lru_bx[l], lru_lambda[l], gn_lru_g[l], gn_attn_g[l], w_out[l],
                         ln_cross_g[l], ln_mem_g[l], w_cq[l], w_ck[l], w_cv[l], w_co[l],
                         ln_ffn_g[l], peer_wq[l], peer_subkeys[l], peer_u[l], peer_v[l])
    return rmsnorm(x, ln_final_g)
```

```python
import functools
import math

import jax
import jax.numpy as jnp
from jax import lax
from jax.experimental import pallas as pl
from jax.experimental.pallas import tpu as pltpu

F32 = jnp.float32
BF16 = jnp.bfloat16

EPS = 1e-6
NEG = -1e30

LRU_BLOCKS = 8
CONV_WIDTH = 4
LRU_C = 8.0
ATTN_HEADS = 8
ATTN_HEAD_DIM = 64
MOBA_BLOCK = 256
MOBA_TOPK = 3
CROSS_HEADS = 4
PEER_KEYS = 128
PEER_HEADS = 8
PEER_TOPK = 16
PEER_HALF = 128

LANES = 128
SUBLANES = 8
VMEM_LIMIT = 48 * 1024 * 1024


def _cparams(sem):
    return pltpu.CompilerParams(dimension_semantics=sem, vmem_limit_bytes=VMEM_LIMIT)


def _rms(x, g):
    return x * lax.rsqrt(jnp.mean(x * x, axis=-1, keepdims=True) + EPS) * g


def _gelu(x):
    return 0.5 * x * (1.0 + lax.erf(x * (1.0 / math.sqrt(2.0))))


def _inproj_kernel(x_ref, g_ref, wl_ref, wq_ref, lru_ref, qkv_ref):
    h = _rms(x_ref[...], g_ref[...]).astype(BF16)
    lru_ref[...] = jnp.dot(h, wl_ref[...], preferred_element_type=F32)
    qkv_ref[...] = jnp.dot(h, wq_ref[...], preferred_element_type=F32)


def _inproj(x2, g, w_lru, w_qkv, tm):
    t, d = x2.shape
    nl, nq = w_lru.shape[1], w_qkv.shape[1]
    return pl.pallas_call(
        _inproj_kernel,
        grid=(t // tm,),
        in_specs=[
            pl.BlockSpec((tm, d), lambda i: (i, 0)),
            pl.BlockSpec((1, d), lambda i: (0, 0)),
            pl.BlockSpec((d, nl), lambda i: (0, 0)),
            pl.BlockSpec((d, nq), lambda i: (0, 0)),
        ],
        out_specs=[
            pl.BlockSpec((tm, nl), lambda i: (i, 0)),
            pl.BlockSpec((tm, nq), lambda i: (i, 0)),
        ],
        out_shape=[jax.ShapeDtypeStruct((t, nl), F32), jax.ShapeDtypeStruct((t, nq), F32)],
        compiler_params=_cparams(("parallel",)),
        name="inproj",
    )(x2, g, w_lru, w_qkv)


def _lru_kernel(lru_ref, cw_ref, cb_ref, wa_ref, ba_ref, wx_ref, bx_ref, lam_ref, gn_ref,
                y_ref, xp_ref, a_ref, b_ref, h_ref, hc_ref, *, ts, c):
    ti = pl.program_id(1)

    @pl.when(ti == 0)
    def _():
        xp_ref[0:SUBLANES, :] = jnp.zeros((SUBLANES, c), F32)
        hc_ref[...] = jnp.zeros((SUBLANES, c), F32)

    x = lru_ref[0, :, 0:c]
    gate = lru_ref[0, :, c:2 * c]
    xp_ref[SUBLANES:SUBLANES + ts, :] = x
    xc = cb_ref[...] + cw_ref[CONV_WIDTH - 1:CONV_WIDTH, :] * x
    for k in range(CONV_WIDTH - 1):
        off = SUBLANES - (CONV_WIDTH - 1) + k
        xc = xc + cw_ref[k:k + 1, :] * xp_ref[off:off + ts, :]
    xp_ref[0:SUBLANES, :] = x[ts - SUBLANES:ts, :]

    xb = xc.astype(BF16)
    gate_r = jax.nn.sigmoid(jnp.dot(xb, wa_ref[...], preferred_element_type=F32) + ba_ref[...])
    gate_i = jax.nn.sigmoid(jnp.dot(xb, wx_ref[...], preferred_element_type=F32) + bx_ref[...])
    z = -lam_ref[...]
    softplus = jnp.maximum(z, 0.0) + jnp.log1p(jnp.exp(-jnp.abs(z)))
    log_a = (-LRU_C * gate_r) * softplus
    a = jnp.exp(log_a)
    one_minus_a2 = -jnp.tanh(log_a) * (a * a + 1.0)
    a_ref[...] = a
    b_ref[...] = jnp.sqrt(one_minus_a2) * (gate_i * xc)

    row = lax.broadcasted_iota(jnp.int32, (SUBLANES, c), 0)

    def chunk(i, hprev):
        r0 = pl.multiple_of(i * SUBLANES, SUBLANES)
        a8 = a_ref[pl.ds(r0, SUBLANES), :]
        b8 = b_ref[pl.ds(r0, SUBLANES), :]
        for s in (1, 2, 4):
            keep = row >= s
            a_sh = pltpu.roll(a8, s, axis=0)
            b_sh = pltpu.roll(b8, s, axis=0)
            b8 = jnp.where(keep, b8 + a8 * b_sh, b8)
            a8 = jnp.where(keep, a8 * a_sh, a8)
        h8 = b8 + a8 * hprev
        h_ref[pl.ds(r0, SUBLANES), :] = h8
        return jnp.broadcast_to(h8[SUBLANES - 1:SUBLANES, :], (SUBLANES, c))

    hlast = lax.fori_loop(0, ts // SUBLANES, chunk, hc_ref[...], unroll=4)
    hc_ref[...] = hlast

    y = h_ref[...] * _gelu(gate)
    y_ref[0] = _rms(y, gn_ref[...])


def _lru(lru3, conv_w, conv_b, wa, ba, wx, bx, lam, gn, ts):
    b, s, c2 = lru3.shape
    c = c2 // 2
    vec = lambda: pl.BlockSpec((1, c), lambda i, j: (0, 0))
    mat = lambda: pl.BlockSpec((c, c), lambda i, j: (0, 0))
    return pl.pallas_call(
        functools.partial(_lru_kernel, ts=ts, c=c),
        grid=(b, s // ts),
        in_specs=[
            pl.BlockSpec((1, ts, c2), lambda i, j: (i, j, 0)),
            pl.BlockSpec((CONV_WIDTH, c), lambda i, j: (0, 0)),
            vec(), mat(), vec(), mat(), vec(), vec(), vec(),
        ],
        out_specs=pl.BlockSpec((1, ts, c), lambda i, j: (i, j, 0)),
        out_shape=jax.ShapeDtypeStruct((b, s, c), F32),
        scratch_shapes=[
            pltpu.VMEM((ts + SUBLANES, c), F32),
            pltpu.VMEM((ts, c), F32),
            pltpu.VMEM((ts, c), F32),
            pltpu.VMEM((ts, c), F32),
            pltpu.VMEM((SUBLANES, c), F32),
        ],
        compiler_params=_cparams(("parallel", "arbitrary")),
        name="lru",
    )(lru3, conv_w, conv_b, wa, ba, wx, bx, lam, gn)


def _moba_kernel(q_ref, k_ref, v_ref, o_ref, qb_ref, kb_ref, vb_ref, sel_ref, ex_ref, *, s):
    hh = pl.program_id(2)
    nb = s // MOBA_BLOCK
    lane = lax.broadcasted_iota(jnp.int32, (1, LANES), 1)
    hmask = (lane // ATTN_HEAD_DIM) == hh
    q = jnp.where(hmask, q_ref[0], 0.0)
    k = k_ref[0]
    kb_ref[...] = k.astype(BF16)
    vb_ref[...] = v_ref[0].astype(BF16)
    qb_ref[...] = q.astype(BF16)
    scale = ATTN_HEAD_DIM ** -0.5

    kmean = jnp.mean(k.reshape(nb, MOBA_BLOCK, LANES), axis=1)
    gate_t = lax.dot_general(kmean, q, (((1,), (1,)), ((), ())),
                             precision=lax.Precision.HIGHEST, preferred_element_type=F32)
    n_io = lax.broadcasted_iota(jnp.int32, (nb, s), 0)
    qblk = lax.broadcasted_iota(jnp.int32, (nb, s), 1) // MOBA_BLOCK
    rank = jnp.zeros((nb, s), jnp.int32)
    for m in range(nb):
        gm = gate_t[m:m + 1, :]
        beats = (gm > gate_t) | ((gm == gate_t) & (m < n_io))
        rank = rank + jnp.where(beats & (m < qblk), 1, 0)
    sel_t = jnp.where((n_io < qblk) & (rank < MOBA_TOPK), 1.0, 0.0)
    sel_t = jnp.concatenate([sel_t, jnp.zeros((LANES - nb, s), F32)], axis=0)
    sel_ref[...] = sel_t.T.astype(BF16)

    e_n = lax.broadcasted_iota(jnp.int32, (LANES, s), 0)
    e_k = lax.broadcasted_iota(jnp.int32, (LANES, s), 1) // MOBA_BLOCK
    ex_ref[...] = jnp.where(e_n == e_k, 1.0, 0.0).astype(BF16)

    for j in range(nb):
        nk = (j + 1) * MOBA_BLOCK
        r0 = j * MOBA_BLOCK
        sc = lax.dot_general(qb_ref[r0:r0 + MOBA_BLOCK, :], kb_ref[0:nk, :], (((1,), (1,)), ((), ())),
                             preferred_element_type=F32) * scale
        past = jnp.dot(sel_ref[r0:r0 + MOBA_BLOCK, :], ex_ref[:, 0:nk], preferred_element_type=F32)
        qpos = lax.broadcasted_iota(jnp.int32, (MOBA_BLOCK, nk), 0) + r0
        kpos = lax.broadcasted_iota(jnp.int32, (MOBA_BLOCK, nk), 1)
        allowed = (past > 0.5) | ((kpos >= r0) & (kpos <= qpos))
        sc = jnp.where(allowed, sc, NEG)
        mx = jnp.max(sc, axis=-1, keepdims=True)
        p = jnp.exp(sc - mx)
        l = jnp.sum(p, axis=-1, keepdims=True)
        o = jnp.dot(p.astype(BF16), vb_ref[0:nk, :], preferred_element_type=F32) / l
        o = jnp.where(hmask, o, 0.0)

        @pl.when(hh == 0)
        def _():
            o_ref[0, r0:r0 + MOBA_BLOCK, :] = o

        @pl.when(hh != 0)
        def _():
            o_ref[0, r0:r0 + MOBA_BLOCK, :] += o


def _moba(qkv3, d_attn):
    b, s, _ = qkv3.shape
    npair = d_attn // LANES
    per = LANES // ATTN_HEAD_DIM
    return pl.pallas_call(
        functools.partial(_moba_kernel, s=s),
        grid=(b, npair, per),
        in_specs=[
            pl.BlockSpec((1, s, LANES), lambda i, p, h: (i, 0, p)),
            pl.BlockSpec((1, s, LANES), lambda i, p, h: (i, 0, npair + p)),
            pl.BlockSpec((1, s, LANES), lambda i, p, h: (i, 0, 2 * npair + p)),
        ],
        out_specs=pl.BlockSpec((1, s, LANES), lambda i, p, h: (i, 0, p)),
        out_shape=jax.ShapeDtypeStruct((b, s, d_attn), F32),
        scratch_shapes=[
            pltpu.VMEM((s, LANES), BF16),
            pltpu.VMEM((s, LANES), BF16),
            pltpu.VMEM((s, LANES), BF16),
            pltpu.VMEM((s, LANES), BF16),
            pltpu.VMEM((LANES, s), BF16),
        ],
        compiler_params=_cparams(("parallel", "parallel", "arbitrary")),
        name="moba",
    )(qkv3, qkv3, qkv3)


def _outproj_kernel(x_ref, yl_ref, ya_ref, ga_ref, w1_ref, w2_ref, o_ref):
    ya = _rms(ya_ref[...], ga_ref[...]).astype(BF16)
    yl = yl_ref[...].astype(BF16)
    o_ref[...] = (x_ref[...] + jnp.dot(yl, w1_ref[...], preferred_element_type=F32)
                  + jnp.dot(ya, w2_ref[...], preferred_element_type=F32))


def _outproj(x2, yl, ya, ga, w1, w2, tm):
    t, d = x2.shape
    c = yl.shape[1]
    return pl.pallas_call(
        _outproj_kernel,
        grid=(t // tm,),
        in_specs=[
            pl.BlockSpec((tm, d), lambda i: (i, 0)),
            pl.BlockSpec((tm, c), lambda i: (i, 0)),
            pl.BlockSpec((tm, c), lambda i: (i, 0)),
            pl.BlockSpec((1, c), lambda i: (0, 0)),
            pl.BlockSpec((c, d), lambda i: (0, 0)),
            pl.BlockSpec((c, d), lambda i: (0, 0)),
        ],
        out_specs=pl.BlockSpec((tm, d), lambda i: (i, 0)),
        out_shape=jax.ShapeDtypeStruct((t, d), F32),
        compiler_params=_cparams(("parallel",)),
        name="outproj",
    )(x2, yl, ya, ga, w1, w2)


def _memkv_kernel(m_ref, g_ref, wk_ref, wv_ref, k_ref, v_ref):
    mn = _rms(m_ref[...], g_ref[...]).astype(BF16)
    k_ref[...] = jnp.dot(mn, wk_ref[...], preferred_element_type=F32).astype(BF16)
    v_ref[...] = jnp.dot(mn, wv_ref[...], preferred_element_type=F32).astype(BF16)


def _memkv(mem2, g, wk, wv, tm):
    t, d = mem2.shape
    return pl.pallas_call(
        _memkv_kernel,
        grid=(t // tm,),
        in_specs=[
            pl.BlockSpec((tm, d), lambda i: (i, 0)),
            pl.BlockSpec((1, d), lambda i: (0, 0)),
            pl.BlockSpec((d, d), lambda i: (0, 0)),
            pl.BlockSpec((d, d), lambda i: (0, 0)),
        ],
        out_specs=[pl.BlockSpec((tm, d), lambda i: (i, 0)), pl.BlockSpec((tm, d), lambda i: (i, 0))],
        out_shape=[jax.ShapeDtypeStruct((t, d), BF16), jax.ShapeDtypeStruct((t, d), BF16)],
        compiler_params=_cparams(("parallel",)),
        name="memkv",
    )(mem2, g, wk, wv)


def _cross_kernel(x_ref, g_ref, wq_ref, k_ref, v_ref, wo_ref, o_ref, *, d):
    x = x_ref[0]
    q = jnp.dot(_rms(x, g_ref[...]).astype(BF16), wq_ref[...], preferred_element_type=F32)
    dh = d // CROSS_HEADS
    scale = dh ** -0.5
    k = k_ref[0]
    v = v_ref[0]
    outs = []
    for h in range(CROSS_HEADS):
        qh = q[:, h * dh:(h + 1) * dh].astype(BF16)
        sc = lax.dot_general(qh, k[:, h * dh:(h + 1) * dh], (((1,), (1,)), ((), ())),
                             preferred_element_type=F32) * scale
        mx = jnp.max(sc, axis=-1, keepdims=True)
        p = jnp.exp(sc - mx)
        l = jnp.sum(p, axis=-1, keepdims=True)
        outs.append(jnp.dot(p.astype(BF16), v[:, h * dh:(h + 1) * dh], preferred_element_type=F32) / l)
    o = jnp.concatenate(outs, axis=-1).astype(BF16)
    o_ref[0] = x + jnp.dot(o, wo_ref[...], preferred_element_type=F32)


def _cross(x3, g, wq, mk3, mv3, wo, tm):
    b, s, d = x3.shape
    m = mk3.shape[1]
    return pl.pallas_call(
        functools.partial(_cross_kernel, d=d),
        grid=(b, s // tm),
        in_specs=[
            pl.BlockSpec((1, tm, d), lambda i, j: (i, j, 0)),
            pl.BlockSpec((1, d), lambda i, j: (0, 0)),
            pl.BlockSpec((d, d), lambda i, j: (0, 0)),
            pl.BlockSpec((1, m, d), lambda i, j: (i, 0, 0)),
            pl.BlockSpec((1, m, d), lambda i, j: (i, 0, 0)),
            pl.BlockSpec((d, d), lambda i, j: (0, 0)),
        ],
        out_specs=pl.BlockSpec((1, tm, d), lambda i, j: (i, j, 0)),
        out_shape=jax.ShapeDtypeStruct((b, s, d), F32),
        compiler_params=_cparams(("parallel", "parallel")),
        name="cross",
    )(x3, g, wq, mk3, mv3, wo)


def _topk_rows(sc, kk, nrow):
    n = sc.shape[1]
    rows = lax.broadcasted_iota(jnp.int32, (nrow, n), 0)
    vals, idxs = [], []
    for _ in range(kk):
        m = jnp.max(sc, axis=0, keepdims=True)
        idx = jnp.min(jnp.where(sc == m, rows, nrow), axis=0, keepdims=True)
        vals.append(m)
        idxs.append(idx)
        sc = jnp.where(rows == idx, -jnp.inf, sc)
    return jnp.concatenate(vals, axis=0), jnp.concatenate(idxs, axis=0)


def _route_kernel(x_ref, g_ref, wq_ref, sk_ref, xn_ref, ids_ref, gates_ref, *, tm):
    xn = _rms(x_ref[...], g_ref[...])
    xn_ref[...] = xn
    pq = jnp.dot(xn.astype(BF16), wq_ref[...], preferred_element_type=F32)
    kk = PEER_TOPK
    ids_all, gates_all = [], []
    for h in range(PEER_HEADS):
        tops = []
        for p in range(2):
            c0 = (h * 2 + p) * PEER_HALF
            qhp = pq[:, c0:c0 + PEER_HALF]
            st = lax.dot_general(sk_ref[h * 2 + p], qhp, (((1,), (1,)), ((), ())),
                                 precision=lax.Precision.HIGHEST, preferred_element_type=F32)
            tops.append(_topk_rows(st, kk, PEER_KEYS))
        (s0, i0), (s1, i1) = tops
        cand_s = jnp.concatenate([s0[a:a + 1, :] + s1 for a in range(kk)], axis=0)
        cand_i = jnp.concatenate([i0[a:a + 1, :] * PEER_KEYS + i1 for a in range(kk)], axis=0)
        rows = lax.broadcasted_iota(jnp.int32, (kk * kk, tm), 0)
        best, ids = [], []
        for _ in range(kk):
            m = jnp.max(cand_s, axis=0, keepdims=True)
            pos = jnp.min(jnp.where(cand_s == m, rows, kk * kk), axis=0, keepdims=True)
            hit = rows == pos
            best.append(m)
            ids.append(jnp.max(jnp.where(hit, cand_i, -1), axis=0, keepdims=True))
            cand_s = jnp.where(hit, -jnp.inf, cand_s)
        best = jnp.concatenate(best, axis=0)
        e = jnp.exp(best - best[0:1, :])
        gates_all.append(e / jnp.sum(e, axis=0, keepdims=True))
        ids_all.append(jnp.concatenate(ids, axis=0))
    ids_ref[...] = jnp.concatenate(ids_all, axis=0).T
    gates_ref[...] = jnp.concatenate(gates_all, axis=0).T


def _route(x2, g, wq, subkeys, tm):
    t, d = x2.shape
    nq = wq.shape[1]
    ne = PEER_HEADS * PEER_TOPK
    return pl.pallas_call(
        functools.partial(_route_kernel, tm=tm),
        grid=(t // tm,),
        in_specs=[
            pl.BlockSpec((tm, d), lambda i: (i, 0)),
            pl.BlockSpec((1, d), lambda i: (0, 0)),
            pl.BlockSpec((d, nq), lambda i: (0, 0)),
            pl.BlockSpec((PEER_HEADS * 2, PEER_KEYS, PEER_HALF), lambda i: (0, 0, 0)),
        ],
        out_specs=[
            pl.BlockSpec((tm, d), lambda i: (i, 0)),
            pl.BlockSpec((tm, ne), lambda i: (i, 0)),
            pl.BlockSpec((tm, ne), lambda i: (i, 0)),
        ],
        out_shape=[
            jax.ShapeDtypeStruct((t, d), F32),
            jax.ShapeDtypeStruct((t, ne), jnp.int32),
            jax.ShapeDtypeStruct((t, ne), F32),
        ],
        compiler_params=_cparams(("parallel",)),
        name="route",
    )(x2, g, wq, subkeys)


def _experts_kernel(ids_ref, idn_ref, x_ref, xn_ref, gates_ref, gf_ref, uv_ref, o_ref,
                    buf_ref, sem_ref, *, tt, d, ne, nsteps):
    i = pl.program_id(0)
    slot = lax.rem(i, 2)

    def issue(idr, sl):
        for t in range(tt):
            def body(e, carry):
                row = idr[t, e]
                pltpu.make_async_copy(uv_ref.at[pl.ds(row, 1), :],
                                      buf_ref.at[sl, pl.ds(t * ne + e, 1), :], sem_ref.at[sl]).start()
                return carry
            lax.fori_loop(0, ne, body, 0, unroll=8)

    @pl.when(i == 0)
    def _():
        issue(ids_ref, 0)

    @pl.when(i + 1 < nsteps)
    def _():
        issue(idn_ref, 1 - slot)

    pltpu.make_async_copy(uv_ref.at[pl.ds(0, tt * ne), :], buf_ref.at[slot], sem_ref.at[slot]).wait()

    ys = []
    for t in range(tt):
        rows = buf_ref[slot, t * ne:(t + 1) * ne, :]
        xt = jnp.broadcast_to(xn_ref[t:t + 1, :], (SUBLANES, d)).astype(BF16)
        act = lax.dot_general(xt, rows[:, 0:d].astype(BF16), (((1,), (1,)), ((), ())),
                              preferred_element_type=F32)
        w = (gates_ref[t:t + 1, :] * _gelu(act)).astype(BF16)
        ys.append(jnp.dot(w, rows[:, d:2 * d].astype(BF16), preferred_element_type=F32)[0:1, :])
    y = jnp.concatenate(ys, axis=0)
    o_ref[...] = _rms(x_ref[...] + y, gf_ref[...])


def _experts(ids, x2, xn, gates, gf, uv, tt):
    t, d = x2.shape
    ne = ids.shape[1]
    nsteps = t // tt
    return pl.pallas_call(
        functools.partial(_experts_kernel, tt=tt, d=d, ne=ne, nsteps=nsteps),
        grid=(nsteps,),
        in_specs=[
            pl.BlockSpec((tt, ne), lambda i: (i, 0), memory_space=pltpu.SMEM),
            pl.BlockSpec((tt, ne), lambda i: (jnp.minimum(i + 1, nsteps - 1), 0), memory_space=pltpu.SMEM),
            pl.BlockSpec((tt, d), lambda i: (i, 0)),
            pl.BlockSpec((tt, d), lambda i: (i, 0)),
            pl.BlockSpec((tt, ne), lambda i: (i, 0)),
            pl.BlockSpec((1, d), lambda i: (0, 0)),
            pl.BlockSpec(memory_space=pl.ANY),
        ],
        out_specs=pl.BlockSpec((tt, d), lambda i: (i, 0)),
        out_shape=jax.ShapeDtypeStruct((t, d), F32),
        scratch_shapes=[
            pltpu.VMEM((2, tt * ne, 2 * d), F32),
            pltpu.SemaphoreType.DMA((2,)),
        ],
        compiler_params=_cparams(("arbitrary",)),
        name="experts",
    )(ids, ids, x2, xn, gates, gf, uv)


def _block_diag(w):
    n, a, b = w.shape
    eye = jnp.eye(n, dtype=w.dtype)
    return (eye[:, None, :, None] * w[:, :, None, :]).reshape(n * a, n * b)


def _layer(x, mem, ln_mix_g, w_in, conv_w, conv_b, lru_wa, lru_ba, lru_wx, lru_bx, lru_lambda,
           gn_lru_g, gn_attn_g, w_out, ln_cross_g, ln_mem_g, w_cq, w_ck, w_cv, w_co, ln_ffn_g,
           peer_wq, peer_subkeys, peer_u, peer_v, ln_final_g, *, tm, ts, tt):
    b, s, d = x.shape
    t = b * s
    c = d // 2
    row = lambda a: a.reshape(1, -1)
    x2 = x.reshape(t, d)

    w_in_b = w_in.astype(BF16)
    lru2, qkv2 = _inproj(x2, row(ln_mix_g), w_in_b[:, :2 * c], w_in_b[:, 2 * c:], tm)

    y_lru = _lru(lru2.reshape(b, s, 2 * c), conv_w, row(conv_b),
                 _block_diag(lru_wa).astype(BF16), row(lru_ba),
                 _block_diag(lru_wx).astype(BF16), row(lru_bx),
                 row(lru_lambda), row(gn_lru_g), ts)
    y_attn = _moba(qkv2.reshape(b, s, 3 * c), c)

    w_out_b = w_out.astype(BF16)
    x1 = _outproj(x2, y_lru.reshape(t, c), y_attn.reshape(t, c), row(gn_attn_g),
                  w_out_b[:c], w_out_b[c:], tm)

    m = mem.shape[1]
    mk, mv = _memkv(mem.reshape(b * m, d), row(ln_mem_g), w_ck.astype(BF16), w_cv.astype(BF16), m)
    x2b = _cross(x1.reshape(b, s, d), row(ln_cross_g), w_cq.astype(BF16),
                 mk.reshape(b, m, d), mv.reshape(b, m, d), w_co.astype(BF16), tm)

    xr = x2b.reshape(t, d)
    xn, ids, gates = _route(xr, row(ln_ffn_g), peer_wq.astype(BF16),
                            peer_subkeys.reshape(PEER_HEADS * 2, PEER_KEYS, PEER_HALF), tm)
    uv = jnp.concatenate([peer_u, peer_v], axis=1)
    out = _experts(ids, xr, xn, gates, row(ln_final_g), uv, tt)
    return out.reshape(b, s, d)


def kernel(x, mem, ln_mix_g, w_in, conv_w, conv_b, lru_wa, lru_ba, lru_wx, lru_bx, lru_lambda, gn_lru_g, gn_attn_g, w_out, ln_cross_g, ln_mem_g, w_cq, w_ck, w_cv, w_co, ln_ffn_g, peer_wq, peer_subkeys, peer_u, peer_v, ln_final_g):
    assert ln_mix_g.shape[0] == 1, "one layer"
    l = 0
    return _layer(x, mem, ln_mix_g[l], w_in[l], conv_w[l], conv_b[l], lru_wa[l], lru_ba[l],
                  lru_wx[l], lru_bx[l], lru_lambda[l], gn_lru_g[l], gn_attn_g[l], w_out[l],
                  ln_cross_g[l], ln_mem_g[l], w_cq[l], w_ck[l], w_cv[l], w_co[l], ln_ffn_g[l],
                  peer_wq[l], peer_subkeys[l], peer_u[l], peer_v[l], ln_final_g,
                  tm=256, ts=512, tt=8)
```

```python
import functools
import math

import jax
import jax.numpy as jnp
from jax import lax
from jax.experimental import pallas as pl
from jax.experimental.pallas import tpu as pltpu

F32 = jnp.float32
BF16 = jnp.bfloat16

EPS = 1e-6
NEG = -1e30

LRU_BLOCKS = 8
CONV_WIDTH = 4
LRU_C = 8.0
ATTN_HEADS = 8
ATTN_HEAD_DIM = 64
MOBA_BLOCK = 256
MOBA_TOPK = 3
CROSS_HEADS = 4
PEER_KEYS = 128
PEER_HEADS = 8
PEER_TOPK = 16
PEER_HALF = 128

LANES = 128
SUBLANES = 8
VMEM_LIMIT = 48 * 1024 * 1024


def _cparams(sem):
    return pltpu.CompilerParams(dimension_semantics=sem, vmem_limit_bytes=VMEM_LIMIT)


def _rms(x, g):
    return x * lax.rsqrt(jnp.mean(x * x, axis=-1, keepdims=True) + EPS) * g


def _gelu(x):
    return 0.5 * x * (1.0 + lax.erf(x * (1.0 / math.sqrt(2.0))))


def _inproj_kernel(x_ref, g_ref, wl_ref, wq_ref, lru_ref, qkv_ref):
    h = _rms(x_ref[...], g_ref[...]).astype(BF16)
    lru_ref[...] = jnp.dot(h, wl_ref[...], preferred_element_type=F32)
    qkv_ref[...] = jnp.dot(h, wq_ref[...], preferred_element_type=F32)


def _inproj(x2, g, w_lru, w_qkv, tm):
    t, d = x2.shape
    nl, nq = w_lru.shape[1], w_qkv.shape[1]
    return pl.pallas_call(
        _inproj_kernel,
        grid=(t // tm,),
        in_specs=[
            pl.BlockSpec((tm, d), lambda i: (i, 0)),
            pl.BlockSpec((1, d), lambda i: (0, 0)),
            pl.BlockSpec((d, nl), lambda i: (0, 0)),
            pl.BlockSpec((d, nq), lambda i: (0, 0)),
        ],
        out_specs=[
            pl.BlockSpec((tm, nl), lambda i: (i, 0)),
            pl.BlockSpec((tm, nq), lambda i: (i, 0)),
        ],
        out_shape=[jax.ShapeDtypeStruct((t, nl), F32), jax.ShapeDtypeStruct((t, nq), F32)],
        compiler_params=_cparams(("parallel",)),
        name="inproj",
    )(x2, g, w_lru, w_qkv)


def _lru_kernel(lru_ref, cw_ref, cb_ref, wa_ref, ba_ref, wx_ref, bx_ref, lam_ref, gn_ref,
                y_ref, xp_ref, a_ref, b_ref, h_ref, hc_ref, *, ts, c):
    ti = pl.program_id(1)

    @pl.when(ti == 0)
    def _():
        xp_ref[0:SUBLANES, :] = jnp.zeros((SUBLANES, c), F32)
        hc_ref[...] = jnp.zeros((SUBLANES, c), F32)

    x = lru_ref[0, :, 0:c]
    gate = lru_ref[0, :, c:2 * c]
    xp_ref[SUBLANES:SUBLANES + ts, :] = x
    xc = cb_ref[...] + cw_ref[CONV_WIDTH - 1:CONV_WIDTH, :] * x
    for k in range(CONV_WIDTH - 1):
        off = SUBLANES - (CONV_WIDTH - 1) + k
        xc = xc + cw_ref[k:k + 1, :] * xp_ref[off:off + ts, :]
    xp_ref[0:SUBLANES, :] = x[ts - SUBLANES:ts, :]

    xb = xc.astype(BF16)
    gate_r = jax.nn.sigmoid(jnp.dot(xb, wa_ref[...], preferred_element_type=F32) + ba_ref[...])
    gate_i = jax.nn.sigmoid(jnp.dot(xb, wx_ref[...], preferred_element_type=F32) + bx_ref[...])
    z = -lam_ref[...]
    softplus = jnp.maximum(z, 0.0) + jnp.log1p(jnp.exp(-jnp.abs(z)))
    log_a = (-LRU_C * gate_r) * softplus
    a = jnp.exp(log_a)
    one_minus_a2 = -jnp.tanh(log_a) * (a * a + 1.0)
    a_ref[...] = a
    b_ref[...] = jnp.sqrt(one_minus_a2) * (gate_i * xc)

    row = lax.broadcasted_iota(jnp.int32, (SUBLANES, c), 0)

    def chunk(i, hprev):
        r0 = pl.multiple_of(i * SUBLANES, SUBLANES)
        a8 = a_ref[pl.ds(r0, SUBLANES), :]
        b8 = b_ref[pl.ds(r0, SUBLANES), :]
        for s in (1, 2, 4):
            keep = row >= s
            a_sh = pltpu.roll(a8, s, axis=0)
            b_sh = pltpu.roll(b8, s, axis=0)
            b8 = jnp.where(keep, b8 + a8 * b_sh, b8)
            a8 = jnp.where(keep, a8 * a_sh, a8)
        h8 = b8 + a8 * hprev
        h_ref[pl.ds(r0, SUBLANES), :] = h8
        return jnp.broadcast_to(h8[SUBLANES - 1:SUBLANES, :], (SUBLANES, c))

    hlast = lax.fori_loop(0, ts // SUBLANES, chunk, hc_ref[...], unroll=4)
    hc_ref[...] = hlast

    y = h_ref[...] * _gelu(gate)
    y_ref[0] = _rms(y, gn_ref[...])


def _lru(lru3, conv_w, conv_b, wa, ba, wx, bx, lam, gn, ts):
    b, s, c2 = lru3.shape
    c = c2 // 2
    vec = lambda: pl.BlockSpec((1, c), lambda i, j: (0, 0))
    mat = lambda: pl.BlockSpec((c, c), lambda i, j: (0, 0))
    return pl.pallas_call(
        functools.partial(_lru_kernel, ts=ts, c=c),
        grid=(b, s // ts),
        in_specs=[
            pl.BlockSpec((1, ts, c2), lambda i, j: (i, j, 0)),
            pl.BlockSpec((CONV_WIDTH, c), lambda i, j: (0, 0)),
            vec(), mat(), vec(), mat(), vec(), vec(), vec(),
        ],
        out_specs=pl.BlockSpec((1, ts, c), lambda i, j: (i, j, 0)),
        out_shape=jax.ShapeDtypeStruct((b, s, c), F32),
        scratch_shapes=[
            pltpu.VMEM((ts + SUBLANES, c), F32),
            pltpu.VMEM((ts, c), F32),
            pltpu.VMEM((ts, c), F32),
            pltpu.VMEM((ts, c), F32),
            pltpu.VMEM((SUBLANES, c), F32),
        ],
        compiler_params=_cparams(("parallel", "arbitrary")),
        name="lru",
    )(lru3, conv_w, conv_b, wa, ba, wx, bx, lam, gn)


def _moba_kernel(q_ref, k_ref, v_ref, o_ref, qb_ref, kb_ref, vb_ref, sel_ref, ex_ref, *, s):
    hh = pl.program_id(2)
    nb = s // MOBA_BLOCK
    lane = lax.broadcasted_iota(jnp.int32, (1, LANES), 1)
    hmask = (lane // ATTN_HEAD_DIM) == hh
    q = jnp.where(hmask, q_ref[0], 0.0)
    k = k_ref[0]
    kb_ref[...] = k.astype(BF16)
    vb_ref[...] = v_ref[0].astype(BF16)
    qb_ref[...] = q.astype(BF16)
    scale = ATTN_HEAD_DIM ** -0.5

    kmean = jnp.mean(k.reshape(nb, MOBA_BLOCK, LANES), axis=1)
    gate_t = lax.dot_general(kmean, q, (((1,), (1,)), ((), ())),
                             precision=lax.Precision.HIGHEST, preferred_element_type=F32)
    n_io = lax.broadcasted_iota(jnp.int32, (nb, s), 0)
    qblk = lax.broadcasted_iota(jnp.int32, (nb, s), 1) // MOBA_BLOCK
    rank = jnp.zeros((nb, s), jnp.int32)
    for m in range(nb):
        gm = gate_t[m:m + 1, :]
        beats = (gm > gate_t) | ((gm == gate_t) & (m < n_io))
        rank = rank + jnp.where(beats & (m < qblk), 1, 0)
    sel_t = jnp.where((n_io < qblk) & (rank < MOBA_TOPK), 1.0, 0.0)
    sel_t = jnp.concatenate([sel_t, jnp.zeros((LANES - nb, s), F32)], axis=0)
    sel_ref[...] = sel_t.T.astype(BF16)

    e_n = lax.broadcasted_iota(jnp.int32, (LANES, s), 0)
    e_k = lax.broadcasted_iota(jnp.int32, (LANES, s), 1) // MOBA_BLOCK
    ex_ref[...] = jnp.where(e_n == e_k, 1.0, 0.0).astype(BF16)

    for j in range(nb):
        nk = (j + 1) * MOBA_BLOCK
        r0 = j * MOBA_BLOCK
        sc = lax.dot_general(qb_ref[r0:r0 + MOBA_BLOCK, :], kb_ref[0:nk, :], (((1,), (1,)), ((), ())),
                             preferred_element_type=F32) * scale
        past = jnp.dot(sel_ref[r0:r0 + MOBA_BLOCK, :], ex_ref[:, 0:nk], preferred_element_type=F32)
        qpos = lax.broadcasted_iota(jnp.int32, (MOBA_BLOCK, nk), 0) + r0
        kpos = lax.broadcasted_iota(jnp.int32, (MOBA_BLOCK, nk), 1)
        allowed = (past > 0.5) | ((kpos >= r0) & (kpos <= qpos))
        sc = jnp.where(allowed, sc, NEG)
        mx = jnp.max(sc, axis=-1, keepdims=True)
        p = jnp.exp(sc - mx)
        l = jnp.sum(p, axis=-1, keepdims=True)
        o = jnp.dot(p.astype(BF16), vb_ref[0:nk, :], preferred_element_type=F32) / l
        o = jnp.where(hmask, o, 0.0)

        @pl.when(hh == 0)
        def _():
            o_ref[0, r0:r0 + MOBA_BLOCK, :] = o

        @pl.when(hh != 0)
        def _():
            o_ref[0, r0:r0 + MOBA_BLOCK, :] += o


def _moba(qkv3, d_attn):
    b, s, _ = qkv3.shape
    npair = d_attn // LANES
    per = LANES // ATTN_HEAD_DIM
    return pl.pallas_call(
        functools.partial(_moba_kernel, s=s),
        grid=(b, npair, per),
        in_specs=[
            pl.BlockSpec((1, s, LANES), lambda i, p, h: (i, 0, p)),
            pl.BlockSpec((1, s, LANES), lambda i, p, h: (i, 0, npair + p)),
            pl.BlockSpec((1, s, LANES), lambda i, p, h: (i, 0, 2 * npair + p)),
        ],
        out_specs=pl.BlockSpec((1, s, LANES), lambda i, p, h: (i, 0, p)),
        out_shape=jax.ShapeDtypeStruct((b, s, d_attn), F32),
        scratch_shapes=[
            pltpu.VMEM((s, LANES), BF16),
            pltpu.VMEM((s, LANES), BF16),
            pltpu.VMEM((s, LANES), BF16),
            pltpu.VMEM((s, LANES), BF16),
            pltpu.VMEM((LANES, s), BF16),
        ],
        compiler_params=_cparams(("parallel", "parallel", "arbitrary")),
        name="moba",
    )(qkv3, qkv3, qkv3)


def _outproj_kernel(x_ref, yl_ref, ya_ref, ga_ref, w1_ref, w2_ref, o_ref):
    ya = _rms(ya_ref[...], ga_ref[...]).astype(BF16)
    yl = yl_ref[...].astype(BF16)
    o_ref[...] = (x_ref[...] + jnp.dot(yl, w1_ref[...], preferred_element_type=F32)
                  + jnp.dot(ya, w2_ref[...], preferred_element_type=F32))


def _outproj(x2, yl, ya, ga, w1, w2, tm):
    t, d = x2.shape
    c = yl.shape[1]
    return pl.pallas_call(
        _outproj_kernel,
        grid=(t // tm,),
        in_specs=[
            pl.BlockSpec((tm, d), lambda i: (i, 0)),
            pl.BlockSpec((tm, c), lambda i: (i, 0)),
            pl.BlockSpec((tm, c), lambda i: (i, 0)),
            pl.BlockSpec((1, c), lambda i: (0, 0)),
            pl.BlockSpec((c, d), lambda i: (0, 0)),
            pl.BlockSpec((c, d), lambda i: (0, 0)),
        ],
        out_specs=pl.BlockSpec((tm, d), lambda i: (i, 0)),
        out_shape=jax.ShapeDtypeStruct((t, d), F32),
        compiler_params=_cparams(("parallel",)),
        name="outproj",
    )(x2, yl, ya, ga, w1, w2)


def _memkv_kernel(m_ref, g_ref, wk_ref, wv_ref, k_ref, v_ref):
    mn = _rms(m_ref[...], g_ref[...]).astype(BF16)
    k_ref[...] = jnp.dot(mn, wk_ref[...], preferred_element_type=F32).astype(BF16)
    v_ref[...] = jnp.dot(mn, wv_ref[...], preferred_element_type=F32).astype(BF16)


def _memkv(mem2, g, wk, wv, tm):
    t, d = mem2.shape
    return pl.pallas_call(
        _memkv_kernel,
        grid=(t // tm,),
        in_specs=[
            pl.BlockSpec((tm, d), lambda i: (i, 0)),
            pl.BlockSpec((1, d), lambda i: (0, 0)),
            pl.BlockSpec((d, d), lambda i: (0, 0)),
            pl.BlockSpec((d, d), lambda i: (0, 0)),
        ],
        out_specs=[pl.BlockSpec((tm, d), lambda i: (i, 0)), pl.BlockSpec((tm, d), lambda i: (i, 0))],
        out_shape=[jax.ShapeDtypeStruct((t, d), BF16), jax.ShapeDtypeStruct((t, d), BF16)],
        compiler_params=_cparams(("parallel",)),
        name="memkv",
    )(mem2, g, wk, wv)


def _cross_kernel(x_ref, g_ref, wq_ref, k_ref, v_ref, wo_ref, o_ref, *, d):
    x = x_ref[0]
    q = jnp.dot(_rms(x, g_ref[...]).astype(BF16), wq_ref[...], preferred_element_type=F32)
    dh = d // CROSS_HEADS
    scale = dh ** -0.5
    k = k_ref[0]
    v = v_ref[0]
    outs = []
    for h in range(CROSS_HEADS):
        qh = q[:, h * dh:(h + 1) * dh].astype(BF16)
        sc = lax.dot_general(qh, k[:, h * dh:(h + 1) * dh], (((1,), (1,)), ((), ())),
                             preferred_element_type=F32) * scale
        mx = jnp.max(sc, axis=-1, keepdims=True)
        p = jnp.exp(sc - mx)
        l = jnp.sum(p, axis=-1, keepdims=True)
        outs.append(jnp.dot(p.astype(BF16), v[:, h * dh:(h + 1) * dh], preferred_element_type=F32) / l)
    o = jnp.concatenate(outs, axis=-1).astype(BF16)
    o_ref[0] = x + jnp.dot(o, wo_ref[...], preferred_element_type=F32)


def _cross(x3, g, wq, mk3, mv3, wo, tm):
    b, s, d = x3.shape
    m = mk3.shape[1]
    return pl.pallas_call(
        functools.partial(_cross_kernel, d=d),
        grid=(b, s // tm),
        in_specs=[
            pl.BlockSpec((1, tm, d), lambda i, j: (i, j, 0)),
            pl.BlockSpec((1, d), lambda i, j: (0, 0)),
            pl.BlockSpec((d, d), lambda i, j: (0, 0)),
            pl.BlockSpec((1, m, d), lambda i, j: (i, 0, 0)),
            pl.BlockSpec((1, m, d), lambda i, j: (i, 0, 0)),
            pl.BlockSpec((d, d), lambda i, j: (0, 0)),
        ],
        out_specs=pl.BlockSpec((1, tm, d), lambda i, j: (i, j, 0)),
        out_shape=jax.ShapeDtypeStruct((b, s, d), F32),
        compiler_params=_cparams(("parallel", "parallel")),
        name="cross",
    )(x3, g, wq, mk3, mv3, wo)


def _topk_rows(sc, kk, nrow):
    n = sc.shape[1]
    rows = lax.broadcasted_iota(jnp.int32, (nrow, n), 0)
    vals, idxs = [], []
    for _ in range(kk):
        m = jnp.max(sc, axis=0, keepdims=True)
        idx = jnp.min(jnp.where(sc == m, rows, nrow), axis=0, keepdims=True)
        vals.append(m)
        idxs.append(idx)
        sc = jnp.where(rows == idx, -jnp.inf, sc)
    return jnp.concatenate(vals, axis=0), jnp.concatenate(idxs, axis=0)


def _route_kernel(x_ref, g_ref, wq_ref, sk_ref, xn_ref, ids_ref, gates_ref, *, tm):
    xn = _rms(x_ref[...], g_ref[...])
    xn_ref[...] = xn
    pq = jnp.dot(xn.astype(BF16), wq_ref[...], preferred_element_type=F32)
    kk = PEER_TOPK
    ids_all, gates_all = [], []
    for h in range(PEER_HEADS):
        tops = []
        for p in range(2):
            c0 = (h * 2 + p) * PEER_HALF
            qhp = pq[:, c0:c0 + PEER_HALF]
            st = lax.dot_general(sk_ref[h * 2 + p], qhp, (((1,), (1,)), ((), ())),
                                 precision=lax.Precision.HIGHEST, preferred_element_type=F32)
            tops.append(_topk_rows(st, kk, PEER_KEYS))
        (s0, i0), (s1, i1) = tops
        cand_s = jnp.concatenate([s0[a:a + 1, :] + s1 for a in range(kk)], axis=0)
        cand_i = jnp.concatenate([i0[a:a + 1, :] * PEER_KEYS + i1 for a in range(kk)], axis=0)
        rows = lax.broadcasted_iota(jnp.int32, (kk * kk, tm), 0)
        best, ids = [], []
        for _ in range(kk):
            m = jnp.max(cand_s, axis=0, keepdims=True)
            pos = jnp.min(jnp.where(cand_s == m, rows, kk * kk), axis=0, keepdims=True)
            hit = rows == pos
            best.append(m)
            ids.append(jnp.max(jnp.where(hit, cand_i, -1), axis=0, keepdims=True))
            cand_s = jnp.where(hit, -jnp.inf, cand_s)
        best = jnp.concatenate(best, axis=0)
        e = jnp.exp(best - best[0:1, :])
        gates_all.append(e / jnp.sum(e, axis=0, keepdims=True))
        ids_all.append(jnp.concatenate(ids, axis=0))
    ids_ref[...] = jnp.concatenate(ids_all, axis=0).T
    gates_ref[...] = jnp.concatenate(gates_all, axis=0).T


def _route(x2, g, wq, subkeys, tm):
    t, d = x2.shape
    nq = wq.shape[1]
    ne = PEER_HEADS * PEER_TOPK
    return pl.pallas_call(
        functools.partial(_route_kernel, tm=tm),
        grid=(t // tm,),
        in_specs=[
            pl.BlockSpec((tm, d), lambda i: (i, 0)),
            pl.BlockSpec((1, d), lambda i: (0, 0)),
            pl.BlockSpec((d, nq), lambda i: (0, 0)),
            pl.BlockSpec((PEER_HEADS * 2, PEER_KEYS, PEER_HALF), lambda i: (0, 0, 0)),
        ],
        out_specs=[
            pl.BlockSpec((tm, d), lambda i: (i, 0)),
            pl.BlockSpec((tm, ne), lambda i: (i, 0)),
            pl.BlockSpec((tm, ne), lambda i: (i, 0)),
        ],
        out_shape=[
            jax.ShapeDtypeStruct((t, d), F32),
            jax.ShapeDtypeStruct((t, ne), jnp.int32),
            jax.ShapeDtypeStruct((t, ne), F32),
        ],
        compiler_params=_cparams(("parallel",)),
        name="route",
    )(x2, g, wq, subkeys)


def _experts_kernel(ids_ref, idn_ref, x_ref, xn_ref, gates_ref, gf_ref, tab_ref, o_ref,
                    buf_ref, y_ref, v_ref, sem_ref, *, tt, d, ne, nsteps):
    i = pl.program_id(0)
    slot = lax.rem(i, 2)
    nch = d // LANES
    tok_rows = ne * nch
    slot_rows = tt * tok_rows

    def issue_token(idr, sl, t, e0=0, e1=ne):
        for e in range(e0, e1):
            src0 = pl.multiple_of(idr[t * ne + e] * nch, nch)
            dst0 = pl.multiple_of(sl * slot_rows + t * tok_rows + e * nch, nch)
            pltpu.make_async_copy(tab_ref.at[pl.ds(src0, nch), :], buf_ref.at[pl.ds(dst0, nch), :],
                                  sem_ref.at[sl]).start(priority=e % 2)

    def wait_slot(sl):
        dst0 = pl.multiple_of(sl * slot_rows, nch)
        pltpu.make_async_copy(tab_ref.at[pl.ds(0, slot_rows), :], buf_ref.at[pl.ds(dst0, slot_rows), :],
                              sem_ref.at[sl]).wait()

    @pl.when(i == 0)
    def _():
        def first(t, carry):
            issue_token(ids_ref, 0, t)
            return carry
        lax.fori_loop(0, tt, first, 0)

    wait_slot(slot)

    def group(g, carry):
        t0 = pl.multiple_of(g * SUBLANES, SUBLANES)
        xg = xn_ref[pl.ds(t0, SUBLANES), :]
        gg = gates_ref[pl.ds(t0, SUBLANES), :]
        def mix(j, act):
            w = (gg[j:j + 1, :] * _gelu(act)).astype(BF16)
            return jnp.concatenate(
                [jnp.dot(w, v_ref[j, c], preferred_element_type=F32)[0:1, :] for c in range(nch)], axis=1)

        ys = []
        prev = None
        for j in range(SUBLANES):
            t = t0 + j
            base = slot * slot_rows + t * tok_rows
            act = None
            for c in range(nch):
                issue_token(idn_ref, 1 - slot, t, c * (ne // nch), (c + 1) * (ne // nch))
                wc = buf_ref[pl.ds(base + c, ne, stride=nch), :]
                uc = pltpu.bitcast(wc & jnp.uint32(0xFFFF0000), F32).astype(BF16)
                v_ref[j, c] = pltpu.bitcast(wc << 16, F32).astype(BF16)
                xc = jnp.broadcast_to(xg[j:j + 1, c * LANES:(c + 1) * LANES], (SUBLANES, LANES))
                part = lax.dot_general(xc.astype(BF16), uc, (((1,), (1,)), ((), ())),
                                       preferred_element_type=F32)
                act = part if act is None else act + part
            if prev is not None:
                ys.append(mix(*prev))
            prev = (j, act)
        ys.append(mix(*prev))
        y_ref[pl.ds(t0, SUBLANES), :] = jnp.concatenate(ys, axis=0)
        return carry

    lax.fori_loop(0, tt // SUBLANES, group, 0)
    o_ref[...] = _rms(x_ref[...] + y_ref[...], gf_ref[...])

    @pl.when(i == nsteps - 1)
    def _():
        wait_slot(1 - slot)


def _pack_uv(u, v):
    n, d = u.shape
    ub = lax.bitcast_convert_type(u.astype(BF16), jnp.uint16).astype(jnp.uint32)
    vb = lax.bitcast_convert_type(v.astype(BF16), jnp.uint16).astype(jnp.uint32)
    return ((ub << 16) | vb).reshape(n * (d // LANES), LANES)


def _experts(ids, x2, xn, gates, gf, table, tt):
    t, d = x2.shape
    ne = ids.shape[1]
    nsteps = t // tt
    nch = d // LANES
    ids = ids.reshape(t * ne)
    return pl.pallas_call(
        functools.partial(_experts_kernel, tt=tt, d=d, ne=ne, nsteps=nsteps),
        grid=(nsteps,),
        in_specs=[
            pl.BlockSpec((tt * ne,), lambda i: (i,), memory_space=pltpu.SMEM),
            pl.BlockSpec((tt * ne,), lambda i: (jnp.minimum(i + 1, nsteps - 1),), memory_space=pltpu.SMEM),
            pl.BlockSpec((tt, d), lambda i: (i, 0)),
            pl.BlockSpec((tt, d), lambda i: (i, 0)),
            pl.BlockSpec((tt, ne), lambda i: (i, 0)),
            pl.BlockSpec((1, d), lambda i: (0, 0)),
            pl.BlockSpec(memory_space=pl.ANY),
        ],
        out_specs=pl.BlockSpec((tt, d), lambda i: (i, 0)),
        out_shape=jax.ShapeDtypeStruct((t, d), F32),
        scratch_shapes=[
            pltpu.VMEM((2 * tt * ne * nch, LANES), jnp.uint32),
            pltpu.VMEM((tt, d), F32),
            pltpu.VMEM((SUBLANES, nch, ne, LANES), BF16),
            pltpu.SemaphoreType.DMA((2,)),
        ],
        compiler_params=_cparams(("arbitrary",)),
        name="experts",
    )(ids, ids, x2, xn, gates, gf, table)


def _block_diag(w):
    n, a, b = w.shape
    eye = jnp.eye(n, dtype=w.dtype)
    return (eye[:, None, :, None] * w[:, :, None, :]).reshape(n * a, n * b)


def _layer(x, mem, ln_mix_g, w_in, conv_w, conv_b, lru_wa, lru_ba, lru_wx, lru_bx, lru_lambda,
           gn_lru_g, gn_attn_g, w_out, ln_cross_g, ln_mem_g, w_cq, w_ck, w_cv, w_co, ln_ffn_g,
           peer_wq, peer_subkeys, peer_u, peer_v, ln_final_g, *, tm, ts, tt):
    b, s, d = x.shape
    t = b * s
    c = d // 2
    row = lambda a: a.reshape(1, -1)
    x2 = x.reshape(t, d)

    w_in_b = w_in.astype(BF16)
    lru2, qkv2 = _inproj(x2, row(ln_mix_g), w_in_b[:, :2 * c], w_in_b[:, 2 * c:], tm)

    y_lru = _lru(lru2.reshape(b, s, 2 * c), conv_w, row(conv_b),
                 _block_diag(lru_wa).astype(BF16), row(lru_ba),
                 _block_diag(lru_wx).astype(BF16), row(lru_bx),
                 row(lru_lambda), row(gn_lru_g), ts)
    y_attn = _moba(qkv2.reshape(b, s, 3 * c), c)

    w_out_b = w_out.astype(BF16)
    x1 = _outproj(x2, y_lru.reshape(t, c), y_attn.reshape(t, c), row(gn_attn_g),
                  w_out_b[:c], w_out_b[c:], tm)

    m = mem.shape[1]
    mk, mv = _memkv(mem.reshape(b * m, d), row(ln_mem_g), w_ck.astype(BF16), w_cv.astype(BF16), m)
    x2b = _cross(x1.reshape(b, s, d), row(ln_cross_g), w_cq.astype(BF16),
                 mk.reshape(b, m, d), mv.reshape(b, m, d), w_co.astype(BF16), tm)

    xr = x2b.reshape(t, d)
    xn, ids, gates = _route(xr, row(ln_ffn_g), peer_wq.astype(BF16),
                            peer_subkeys.reshape(PEER_HEADS * 2, PEER_KEYS, PEER_HALF), tm)
    out = _experts(ids, xr, xn, gates, row(ln_final_g), _pack_uv(peer_u, peer_v), tt)
    return out.reshape(b, s, d)


def kernel(x, mem, ln_mix_g, w_in, conv_w, conv_b, lru_wa, lru_ba, lru_wx, lru_bx, lru_lambda, gn_lru_g, gn_attn_g, w_out, ln_cross_g, ln_mem_g, w_cq, w_ck, w_cv, w_co, ln_ffn_g, peer_wq, peer_subkeys, peer_u, peer_v, ln_final_g):
    assert ln_mix_g.shape[0] == 1, "one layer"
    l = 0
    return _layer(x, mem, ln_mix_g[l], w_in[l], conv_w[l], conv_b[l], lru_wa[l], lru_ba[l],
                  lru_wx[l], lru_bx[l], lru_lambda[l], gn_lru_g[l], gn_attn_g[l], w_out[l],
                  ln_cross_g[l], ln_mem_g[l], w_cq[l], w_ck[l], w_cv[l], w_co[l], ln_ffn_g[l],
                  peer_wq[l], peer_subkeys[l], peer_u[l], peer_v[l], ln_final_g,
                  tm=256, ts=512, tt=16)
```

```python
import functools
import math

import jax
import jax.numpy as jnp
from jax import lax
from jax.experimental import pallas as pl
from jax.experimental.pallas import tpu as pltpu

F32 = jnp.float32
BF16 = jnp.bfloat16

EPS = 1e-6
NEG = -1e30

LRU_BLOCKS = 8
CONV_WIDTH = 4
LRU_C = 8.0
ATTN_HEADS = 8
ATTN_HEAD_DIM = 64
MOBA_BLOCK = 256
MOBA_TOPK = 3
CROSS_HEADS = 4
PEER_KEYS = 128
PEER_HEADS = 8
PEER_TOPK = 16
PEER_HALF = 128

LANES = 128
SUBLANES = 8
VMEM_LIMIT = 48 * 1024 * 1024


def _cparams(sem):
    return pltpu.CompilerParams(dimension_semantics=sem, vmem_limit_bytes=VMEM_LIMIT)


def _rms(x, g):
    return x * lax.rsqrt(jnp.mean(x * x, axis=-1, keepdims=True) + EPS) * g


def _gelu(x):
    return 0.5 * x * (1.0 + lax.erf(x * (1.0 / math.sqrt(2.0))))


def _inproj_kernel(x_ref, g_ref, wl_ref, wq_ref, lru_ref, qkv_ref):
    h = _rms(x_ref[...], g_ref[...]).astype(BF16)
    lru_ref[...] = jnp.dot(h, wl_ref[...], preferred_element_type=F32)
    qkv_ref[...] = jnp.dot(h, wq_ref[...], preferred_element_type=F32)


def _inproj(x2, g, w_lru, w_qkv, tm):
    t, d = x2.shape
    nl, nq = w_lru.shape[1], w_qkv.shape[1]
    return pl.pallas_call(
        _inproj_kernel,
        grid=(t // tm,),
        in_specs=[
            pl.BlockSpec((tm, d), lambda i: (i, 0)),
            pl.BlockSpec((1, d), lambda i: (0, 0)),
            pl.BlockSpec((d, nl), lambda i: (0, 0)),
            pl.BlockSpec((d, nq), lambda i: (0, 0)),
        ],
        out_specs=[
            pl.BlockSpec((tm, nl), lambda i: (i, 0)),
            pl.BlockSpec((tm, nq), lambda i: (i, 0)),
        ],
        out_shape=[jax.ShapeDtypeStruct((t, nl), F32), jax.ShapeDtypeStruct((t, nq), F32)],
        compiler_params=_cparams(("parallel",)),
        name="inproj",
    )(x2, g, w_lru, w_qkv)


def _lru_kernel(lru_ref, cw_ref, cb_ref, wa_ref, ba_ref, wx_ref, bx_ref, lam_ref, gn_ref,
                y_ref, xp_ref, a_ref, b_ref, h_ref, hc_ref, *, ts, c):
    ti = pl.program_id(1)

    @pl.when(ti == 0)
    def _():
        xp_ref[0:SUBLANES, :] = jnp.zeros((SUBLANES, c), F32)
        hc_ref[...] = jnp.zeros((SUBLANES, c), F32)

    x = lru_ref[0, :, 0:c]
    gate = lru_ref[0, :, c:2 * c]
    xp_ref[SUBLANES:SUBLANES + ts, :] = x
    xc = cb_ref[...] + cw_ref[CONV_WIDTH - 1:CONV_WIDTH, :] * x
    for k in range(CONV_WIDTH - 1):
        off = SUBLANES - (CONV_WIDTH - 1) + k
        xc = xc + cw_ref[k:k + 1, :] * xp_ref[off:off + ts, :]
    xp_ref[0:SUBLANES, :] = x[ts - SUBLANES:ts, :]

    xb = xc.astype(BF16)
    gate_r = jax.nn.sigmoid(jnp.dot(xb, wa_ref[...], preferred_element_type=F32) + ba_ref[...])
    gate_i = jax.nn.sigmoid(jnp.dot(xb, wx_ref[...], preferred_element_type=F32) + bx_ref[...])
    z = -lam_ref[...]
    softplus = jnp.maximum(z, 0.0) + jnp.log1p(jnp.exp(-jnp.abs(z)))
    log_a = (-LRU_C * gate_r) * softplus
    a = jnp.exp(log_a)
    one_minus_a2 = -jnp.tanh(log_a) * (a * a + 1.0)
    a_ref[...] = a
    b_ref[...] = jnp.sqrt(one_minus_a2) * (gate_i * xc)

    row = lax.broadcasted_iota(jnp.int32, (SUBLANES, c), 0)

    def chunk(i, hprev):
        r0 = pl.multiple_of(i * SUBLANES, SUBLANES)
        a8 = a_ref[pl.ds(r0, SUBLANES), :]
        b8 = b_ref[pl.ds(r0, SUBLANES), :]
        for s in (1, 2, 4):
            keep = row >= s
            a_sh = pltpu.roll(a8, s, axis=0)
            b_sh = pltpu.roll(b8, s, axis=0)
            b8 = jnp.where(keep, b8 + a8 * b_sh, b8)
            a8 = jnp.where(keep, a8 * a_sh, a8)
        h8 = b8 + a8 * hprev
        h_ref[pl.ds(r0, SUBLANES), :] = h8
        return jnp.broadcast_to(h8[SUBLANES - 1:SUBLANES, :], (SUBLANES, c))

    hlast = lax.fori_loop(0, ts // SUBLANES, chunk, hc_ref[...], unroll=4)
    hc_ref[...] = hlast

    y = h_ref[...] * _gelu(gate)
    y_ref[0] = _rms(y, gn_ref[...])


def _lru(lru3, conv_w, conv_b, wa, ba, wx, bx, lam, gn, ts):
    b, s, c2 = lru3.shape
    c = c2 // 2
    vec = lambda: pl.BlockSpec((1, c), lambda i, j: (0, 0))
    mat = lambda: pl.BlockSpec((c, c), lambda i, j: (0, 0))
    return pl.pallas_call(
        functools.partial(_lru_kernel, ts=ts, c=c),
        grid=(b, s // ts),
        in_specs=[
            pl.BlockSpec((1, ts, c2), lambda i, j: (i, j, 0)),
            pl.BlockSpec((CONV_WIDTH, c), lambda i, j: (0, 0)),
            vec(), mat(), vec(), mat(), vec(), vec(), vec(),
        ],
        out_specs=pl.BlockSpec((1, ts, c), lambda i, j: (i, j, 0)),
        out_shape=jax.ShapeDtypeStruct((b, s, c), F32),
        scratch_shapes=[
            pltpu.VMEM((ts + SUBLANES, c), F32),
            pltpu.VMEM((ts, c), F32),
            pltpu.VMEM((ts, c), F32),
            pltpu.VMEM((ts, c), F32),
            pltpu.VMEM((SUBLANES, c), F32),
        ],
        compiler_params=_cparams(("parallel", "arbitrary")),
        name="lru",
    )(lru3, conv_w, conv_b, wa, ba, wx, bx, lam, gn)


def _moba_kernel(q_ref, k_ref, v_ref, o_ref, qb_ref, kb_ref, vb_ref, sel_ref, ex_ref, *, s):
    hh = pl.program_id(2)
    nb = s // MOBA_BLOCK
    lane = lax.broadcasted_iota(jnp.int32, (1, LANES), 1)
    hmask = (lane // ATTN_HEAD_DIM) == hh
    q = jnp.where(hmask, q_ref[0], 0.0)
    k = k_ref[0]
    kb_ref[...] = k.astype(BF16)
    vb_ref[...] = v_ref[0].astype(BF16)
    scale = ATTN_HEAD_DIM ** -0.5
    assert math.frexp(scale)[0] == 0.5, "folding the scale into q is exact only for a power of two"
    qb_ref[...] = (q * scale).astype(BF16)

    kmean = jnp.mean(k.reshape(nb, MOBA_BLOCK, LANES), axis=1)
    gate_t = lax.dot_general(kmean, q, (((1,), (1,)), ((), ())),
                             precision=lax.Precision.HIGHEST, preferred_element_type=F32)
    n_io = lax.broadcasted_iota(jnp.int32, (nb, s), 0)
    qblk = lax.broadcasted_iota(jnp.int32, (nb, s), 1) // MOBA_BLOCK
    rank = jnp.zeros((nb, s), jnp.int32)
    for m in range(nb):
        gm = gate_t[m:m + 1, :]
        beats = (gm > gate_t) | ((gm == gate_t) & (m < n_io))
        rank = rank + jnp.where(beats & (m < qblk), 1, 0)
    sel_t = jnp.where((n_io < qblk) & (rank < MOBA_TOPK), 1.0, 0.0)
    sel_t = jnp.concatenate([sel_t, jnp.zeros((LANES - nb, s), F32)], axis=0)
    sel_ref[...] = sel_t.T.astype(BF16)

    e_n = lax.broadcasted_iota(jnp.int32, (LANES, s), 0)
    e_k = lax.broadcasted_iota(jnp.int32, (LANES, s), 1) // MOBA_BLOCK
    ex_ref[...] = jnp.where(e_n == e_k, 1.0, 0.0).astype(BF16)

    causal = (lax.broadcasted_iota(jnp.int32, (MOBA_BLOCK, MOBA_BLOCK), 1)
              <= lax.broadcasted_iota(jnp.int32, (MOBA_BLOCK, MOBA_BLOCK), 0))
    for j in range(nb):
        nk = (j + 1) * MOBA_BLOCK
        r0 = j * MOBA_BLOCK
        qj = qb_ref[r0:r0 + MOBA_BLOCK, :]
        own = lax.dot_general(qj, kb_ref[r0:nk, :], (((1,), (1,)), ((), ())), preferred_element_type=F32)
        own = jnp.where(causal, own, NEG)
        if j == 0:
            sc = own
        else:
            pst = lax.dot_general(qj, kb_ref[0:r0, :], (((1,), (1,)), ((), ())), preferred_element_type=F32)
            flag = jnp.dot(sel_ref[r0:r0 + MOBA_BLOCK, :], ex_ref[:, 0:r0], preferred_element_type=F32)
            sc = jnp.concatenate([jnp.where(flag > 0.5, pst, NEG), own], axis=1)
        mx = jnp.max(sc, axis=-1, keepdims=True)
        p = jnp.exp(sc - mx)
        l = jnp.sum(p, axis=-1, keepdims=True)
        o = jnp.dot(p.astype(BF16), vb_ref[0:nk, :], preferred_element_type=F32) / l
        o = jnp.where(hmask, o, 0.0)

        @pl.when(hh == 0)
        def _():
            o_ref[0, r0:r0 + MOBA_BLOCK, :] = o

        @pl.when(hh != 0)
        def _():
            o_ref[0, r0:r0 + MOBA_BLOCK, :] += o


def _moba(qkv3, d_attn):
    b, s, _ = qkv3.shape
    npair = d_attn // LANES
    per = LANES // ATTN_HEAD_DIM
    return pl.pallas_call(
        functools.partial(_moba_kernel, s=s),
        grid=(b, npair, per),
        in_specs=[
            pl.BlockSpec((1, s, LANES), lambda i, p, h: (i, 0, p)),
            pl.BlockSpec((1, s, LANES), lambda i, p, h: (i, 0, npair + p)),
            pl.BlockSpec((1, s, LANES), lambda i, p, h: (i, 0, 2 * npair + p)),
        ],
        out_specs=pl.BlockSpec((1, s, LANES), lambda i, p, h: (i, 0, p)),
        out_shape=jax.ShapeDtypeStruct((b, s, d_attn), F32),
        scratch_shapes=[
            pltpu.VMEM((s, LANES), BF16),
            pltpu.VMEM((s, LANES), BF16),
            pltpu.VMEM((s, LANES), BF16),
            pltpu.VMEM((s, LANES), BF16),
            pltpu.VMEM((LANES, s), BF16),
        ],
        compiler_params=_cparams(("parallel", "parallel", "arbitrary")),
        name="moba",
    )(qkv3, qkv3, qkv3)


def _outproj_kernel(x_ref, yl_ref, ya_ref, ga_ref, w1_ref, w2_ref, o_ref):
    ya = _rms(ya_ref[...], ga_ref[...]).astype(BF16)
    yl = yl_ref[...].astype(BF16)
    o_ref[...] = (x_ref[...] + jnp.dot(yl, w1_ref[...], preferred_element_type=F32)
                  + jnp.dot(ya, w2_ref[...], preferred_element_type=F32))


def _outproj(x2, yl, ya, ga, w1, w2, tm):
    t, d = x2.shape
    c = yl.shape[1]
    return pl.pallas_call(
        _outproj_kernel,
        grid=(t // tm,),
        in_specs=[
            pl.BlockSpec((tm, d), lambda i: (i, 0)),
            pl.BlockSpec((tm, c), lambda i: (i, 0)),
            pl.BlockSpec((tm, c), lambda i: (i, 0)),
            pl.BlockSpec((1, c), lambda i: (0, 0)),
            pl.BlockSpec((c, d), lambda i: (0, 0)),
            pl.BlockSpec((c, d), lambda i: (0, 0)),
        ],
        out_specs=pl.BlockSpec((tm, d), lambda i: (i, 0)),
        out_shape=jax.ShapeDtypeStruct((t, d), F32),
        compiler_params=_cparams(("parallel",)),
        name="outproj",
    )(x2, yl, ya, ga, w1, w2)


def _memkv_kernel(m_ref, g_ref, wk_ref, wv_ref, k_ref, v_ref):
    mn = _rms(m_ref[...], g_ref[...]).astype(BF16)
    k_ref[...] = jnp.dot(mn, wk_ref[...], preferred_element_type=F32).astype(BF16)
    v_ref[...] = jnp.dot(mn, wv_ref[...], preferred_element_type=F32).astype(BF16)


def _memkv(mem2, g, wk, wv, tm):
    t, d = mem2.shape
    return pl.pallas_call(
        _memkv_kernel,
        grid=(t // tm,),
        in_specs=[
            pl.BlockSpec((tm, d), lambda i: (i, 0)),
            pl.BlockSpec((1, d), lambda i: (0, 0)),
            pl.BlockSpec((d, d), lambda i: (0, 0)),
            pl.BlockSpec((d, d), lambda i: (0, 0)),
        ],
        out_specs=[pl.BlockSpec((tm, d), lambda i: (i, 0)), pl.BlockSpec((tm, d), lambda i: (i, 0))],
        out_shape=[jax.ShapeDtypeStruct((t, d), BF16), jax.ShapeDtypeStruct((t, d), BF16)],
        compiler_params=_cparams(("parallel",)),
        name="memkv",
    )(mem2, g, wk, wv)


def _cross_kernel(x_ref, g_ref, wq_ref, k_ref, v_ref, wo_ref, o_ref, *, d):
    x = x_ref[0]
    q = jnp.dot(_rms(x, g_ref[...]).astype(BF16), wq_ref[...], preferred_element_type=F32)
    dh = d // CROSS_HEADS
    scale = dh ** -0.5
    k = k_ref[0]
    v = v_ref[0]
    outs = []
    for h in range(CROSS_HEADS):
        qh = q[:, h * dh:(h + 1) * dh].astype(BF16)
        sc = lax.dot_general(qh, k[:, h * dh:(h + 1) * dh], (((1,), (1,)), ((), ())),
                             preferred_element_type=F32) * scale
        mx = jnp.max(sc, axis=-1, keepdims=True)
        p = jnp.exp(sc - mx)
        l = jnp.sum(p, axis=-1, keepdims=True)
        outs.append(jnp.dot(p.astype(BF16), v[:, h * dh:(h + 1) * dh], preferred_element_type=F32) / l)
    o = jnp.concatenate(outs, axis=-1).astype(BF16)
    o_ref[0] = x + jnp.dot(o, wo_ref[...], preferred_element_type=F32)


def _cross(x3, g, wq, mk3, mv3, wo, tm):
    b, s, d = x3.shape
    m = mk3.shape[1]
    return pl.pallas_call(
        functools.partial(_cross_kernel, d=d),
        grid=(b, s // tm),
        in_specs=[
            pl.BlockSpec((1, tm, d), lambda i, j: (i, j, 0)),
            pl.BlockSpec((1, d), lambda i, j: (0, 0)),
            pl.BlockSpec((d, d), lambda i, j: (0, 0)),
            pl.BlockSpec((1, m, d), lambda i, j: (i, 0, 0)),
            pl.BlockSpec((1, m, d), lambda i, j: (i, 0, 0)),
            pl.BlockSpec((d, d), lambda i, j: (0, 0)),
        ],
        out_specs=pl.BlockSpec((1, tm, d), lambda i, j: (i, j, 0)),
        out_shape=jax.ShapeDtypeStruct((b, s, d), F32),
        compiler_params=_cparams(("parallel", "parallel")),
        name="cross",
    )(x3, g, wq, mk3, mv3, wo)


def _topk_rows(sc, kk, nrow):
    n = sc.shape[1]
    rows = lax.broadcasted_iota(jnp.int32, (nrow, n), 0)
    vals, idxs = [], []
    for _ in range(kk):
        m = jnp.max(sc, axis=0, keepdims=True)
        idx = jnp.min(jnp.where(sc == m, rows, nrow), axis=0, keepdims=True)
        vals.append(m)
        idxs.append(idx)
        sc = jnp.where(rows == idx, -jnp.inf, sc)
    return jnp.concatenate(vals, axis=0), jnp.concatenate(idxs, axis=0)


def _route_kernel(x_ref, g_ref, wq_ref, sk_ref, xn_ref, ids_ref, gates_ref, *, tm):
    xn = _rms(x_ref[...], g_ref[...])
    xn_ref[...] = xn
    pq = jnp.dot(xn.astype(BF16), wq_ref[...], preferred_element_type=F32)
    kk = PEER_TOPK
    assert kk == 2 * SUBLANES, "candidate layout below is written for 16 = 2 x 8 sublanes"
    ids_all, gates_all = [], []
    for h in range(PEER_HEADS):
        tops = []
        for p in range(2):
            c0 = (h * 2 + p) * PEER_HALF
            qhp = pq[:, c0:c0 + PEER_HALF]
            st = lax.dot_general(sk_ref[h * 2 + p], qhp, (((1,), (1,)), ((), ())),
                                 precision=lax.Precision.HIGHEST, preferred_element_type=F32)
            tops.append(_topk_rows(st, kk, PEER_KEYS))
        (s0, i0), (s1, i1) = tops
        sub = lax.broadcasted_iota(jnp.int32, (SUBLANES, tm), 0)
        ps, pi = [s0[0:1, :] + s1], [i0[0:1, :] * PEER_KEYS + i1]
        for a in range(1, SUBLANES):
            nb = kk // (a + 1)
            ps.append(jnp.where(sub < nb, s0[a:a + 1, :] + s1[0:SUBLANES, :], -jnp.inf))
            pi.append(i0[a:a + 1, :] * PEER_KEYS + i1[0:SUBLANES, :])
        ps.append(s0[SUBLANES:kk, :] + s1[0:1, :])
        pi.append(i0[SUBLANES:kk, :] * PEER_KEYS + i1[0:1, :])
        cand_s = jnp.concatenate(ps, axis=0)
        cand_i = jnp.concatenate(pi, axis=0)
        ncand = cand_s.shape[0]
        rows = lax.broadcasted_iota(jnp.int32, (ncand, tm), 0)
        best, ids = [], []
        for _ in range(kk):
            m = jnp.max(cand_s, axis=0, keepdims=True)
            pos = jnp.min(jnp.where(cand_s == m, rows, ncand), axis=0, keepdims=True)
            hit = rows == pos
            best.append(m)
            ids.append(jnp.max(jnp.where(hit, cand_i, -1), axis=0, keepdims=True))
            cand_s = jnp.where(hit, -jnp.inf, cand_s)
        best = jnp.concatenate(best, axis=0)
        e = jnp.exp(best - best[0:1, :])
        gates_all.append(e / jnp.sum(e, axis=0, keepdims=True))
        ids_all.append(jnp.concatenate(ids, axis=0))
    ids_ref[...] = jnp.concatenate(ids_all, axis=0).T
    gates_ref[...] = jnp.concatenate(gates_all, axis=0).T


def _route(x2, g, wq, subkeys, tm):
    t, d = x2.shape
    nq = wq.shape[1]
    ne = PEER_HEADS * PEER_TOPK
    return pl.pallas_call(
        functools.partial(_route_kernel, tm=tm),
        grid=(t // tm,),
        in_specs=[
            pl.BlockSpec((tm, d), lambda i: (i, 0)),
            pl.BlockSpec((1, d), lambda i: (0, 0)),
            pl.BlockSpec((d, nq), lambda i: (0, 0)),
            pl.BlockSpec((PEER_HEADS * 2, PEER_KEYS, PEER_HALF), lambda i: (0, 0, 0)),
        ],
        out_specs=[
            pl.BlockSpec((tm, d), lambda i: (i, 0)),
            pl.BlockSpec((tm, ne), lambda i: (i, 0)),
            pl.BlockSpec((tm, ne), lambda i: (i, 0)),
        ],
        out_shape=[
            jax.ShapeDtypeStruct((t, d), F32),
            jax.ShapeDtypeStruct((t, ne), jnp.int32),
            jax.ShapeDtypeStruct((t, ne), F32),
        ],
        compiler_params=_cparams(("parallel",)),
        name="route",
    )(x2, g, wq, subkeys)


ROW_PITCH = 20


def _experts_kernel(ids_ref, idn_ref, x_ref, xn_ref, gates_ref, gf_ref, tab_ref, o_ref,
                    buf_ref, y_ref, sem_ref, *, tt, d, ne, nsteps):
    i = pl.program_id(0)
    slot = lax.rem(i, 2)
    nch = d // LANES
    nrow = 2 * nch
    per = ne // nrow
    tok_rows = ne * ROW_PITCH
    slot_rows = tt * tok_rows
    contract_last = (((1,), (1,)), ((), ()))

    def issue(idr, sl, t, e0, e1):
        for e in range(e0, e1):
            src0 = pl.multiple_of(idr[t * ne + e] * nrow, nrow)
            dst0 = pl.multiple_of(sl * slot_rows + t * tok_rows + e * ROW_PITCH, 4)
            pltpu.make_async_copy(tab_ref.at[pl.ds(src0, nrow), :], buf_ref.at[pl.ds(dst0, nrow), :],
                                  sem_ref.at[sl]).start(priority=e % 2)

    def wait_slot(sl):
        n = tt * ne * nrow
        dst0 = pl.multiple_of(sl * slot_rows, nrow)
        pltpu.make_async_copy(tab_ref.at[pl.ds(0, n), :], buf_ref.at[pl.ds(dst0, n), :],
                              sem_ref.at[sl]).wait()

    def chunk(t, r):
        base = slot * slot_rows + t * tok_rows
        return buf_ref[pl.ds(base + r, ne, stride=ROW_PITCH), :].astype(BF16)

    @pl.when(i == 0)
    def _():
        def first(t, carry):
            issue(ids_ref, 0, t, 0, ne)
            return carry
        lax.fori_loop(0, tt, first, 0)

    wait_slot(slot)

    def group(g, carry):
        t0 = pl.multiple_of(g * SUBLANES, SUBLANES)
        xg = xn_ref[pl.ds(t0, SUBLANES), :]
        gg = gates_ref[pl.ds(t0, SUBLANES), :]

        def value_chunk(j, w, c):
            return jnp.dot(w, chunk(t0 + j, nch + c), preferred_element_type=F32)[0:1, :]

        ys = []
        prev = None
        for j in range(SUBLANES):
            t = t0 + j
            step = 2 * per if prev is None else per
            act = None
            for c in range(nch):
                issue(idn_ref, 1 - slot, t, c * step, (c + 1) * step)
                xc = jnp.broadcast_to(xg[j:j + 1, c * LANES:(c + 1) * LANES], (SUBLANES, LANES))
                part = lax.dot_general(xc.astype(BF16), chunk(t, c), contract_last,
                                       preferred_element_type=F32)
                act = part if act is None else act + part
            if prev is not None:
                w = (gg[prev[0]:prev[0] + 1, :] * _gelu(prev[1])).astype(BF16)
                yprev = []
                for c in range(nch):
                    issue(idn_ref, 1 - slot, t, (nch + c) * step, (nch + c + 1) * step)
                    yprev.append(value_chunk(prev[0], w, c))
                ys.append(jnp.concatenate(yprev, axis=1))
            prev = (j, act)
        w = (gg[prev[0]:prev[0] + 1, :] * _gelu(prev[1])).astype(BF16)
        ys.append(jnp.concatenate([value_chunk(prev[0], w, c) for c in range(nch)], axis=1))
        y_ref[pl.ds(t0, SUBLANES), :] = jnp.concatenate(ys, axis=0)
        return carry

    lax.fori_loop(0, tt // SUBLANES, group, 0)
    o_ref[...] = _rms(x_ref[...] + y_ref[...], gf_ref[...])

    @pl.when(i == nsteps - 1)
    def _():
        wait_slot(1 - slot)


def _stack_uv(u, v):
    n, d = u.shape
    nch = d // LANES
    uv = jnp.concatenate([u.reshape(n, nch, LANES), v.reshape(n, nch, LANES)], axis=1)
    return uv.reshape(n * 2 * nch, LANES)


def _experts(ids, x2, xn, gates, gf, table, tt):
    t, d = x2.shape
    ne = ids.shape[1]
    nsteps = t // tt
    ids = ids.reshape(t * ne)
    return pl.pallas_call(
        functools.partial(_experts_kernel, tt=tt, d=d, ne=ne, nsteps=nsteps),
        grid=(nsteps,),
        in_specs=[
            pl.BlockSpec((tt * ne,), lambda i: (i,), memory_space=pltpu.SMEM),
            pl.BlockSpec((tt * ne,), lambda i: (jnp.minimum(i + 1, nsteps - 1),), memory_space=pltpu.SMEM),
            pl.BlockSpec((tt, d), lambda i: (i, 0)),
            pl.BlockSpec((tt, d), lambda i: (i, 0)),
            pl.BlockSpec((tt, ne), lambda i: (i, 0)),
            pl.BlockSpec((1, d), lambda i: (0, 0)),
            pl.BlockSpec(memory_space=pl.ANY),
        ],
        out_specs=pl.BlockSpec((tt, d), lambda i: (i, 0)),
        out_shape=jax.ShapeDtypeStruct((t, d), F32),
        scratch_shapes=[
            pltpu.VMEM((2 * tt * ne * ROW_PITCH, LANES), F32),
            pltpu.VMEM((tt, d), F32),
            pltpu.SemaphoreType.DMA((2,)),
        ],
        compiler_params=_cparams(("arbitrary",)),
        name="experts",
    )(ids, ids, x2, xn, gates, gf, table)


def _block_diag(w):
    n, a, b = w.shape
    eye = jnp.eye(n, dtype=w.dtype)
    return (eye[:, None, :, None] * w[:, :, None, :]).reshape(n * a, n * b)


def _layer(x, mem, ln_mix_g, w_in, conv_w, conv_b, lru_wa, lru_ba, lru_wx, lru_bx, lru_lambda,
           gn_lru_g, gn_attn_g, w_out, ln_cross_g, ln_mem_g, w_cq, w_ck, w_cv, w_co, ln_ffn_g,
           peer_wq, peer_subkeys, peer_u, peer_v, ln_final_g, *, tm, ts, tt):
    b, s, d = x.shape
    t = b * s
    c = d // 2
    row = lambda a: a.reshape(1, -1)
    x2 = x.reshape(t, d)

    w_in_b = w_in.astype(BF16)
    lru2, qkv2 = _inproj(x2, row(ln_mix_g), w_in_b[:, :2 * c], w_in_b[:, 2 * c:], tm)

    y_lru = _lru(lru2.reshape(b, s, 2 * c), conv_w, row(conv_b),
                 _block_diag(lru_wa).astype(BF16), row(lru_ba),
                 _block_diag(lru_wx).astype(BF16), row(lru_bx),
                 row(lru_lambda), row(gn_lru_g), ts)
    y_attn = _moba(qkv2.reshape(b, s, 3 * c), c)

    w_out_b = w_out.astype(BF16)
    x1 = _outproj(x2, y_lru.reshape(t, c), y_attn.reshape(t, c), row(gn_attn_g),
                  w_out_b[:c], w_out_b[c:], tm)

    m = mem.shape[1]
    mk, mv = _memkv(mem.reshape(b * m, d), row(ln_mem_g), w_ck.astype(BF16), w_cv.astype(BF16), m)
    x2b = _cross(x1.reshape(b, s, d), row(ln_cross_g), w_cq.astype(BF16),
                 mk.reshape(b, m, d), mv.reshape(b, m, d), w_co.astype(BF16), tm)

    xr = x2b.reshape(t, d)
    xn, ids, gates = _route(xr, row(ln_ffn_g), peer_wq.astype(BF16),
                            peer_subkeys.reshape(PEER_HEADS * 2, PEER_KEYS, PEER_HALF), tm)
    out = _experts(ids, xr, xn, gates, row(ln_final_g), _stack_uv(peer_u, peer_v), tt)
    return out.reshape(b, s, d)


def kernel(x, mem, ln_mix_g, w_in, conv_w, conv_b, lru_wa, lru_ba, lru_wx, lru_bx, lru_lambda, gn_lru_g, gn_attn_g, w_out, ln_cross_g, ln_mem_g, w_cq, w_ck, w_cv, w_co, ln_ffn_g, peer_wq, peer_subkeys, peer_u, peer_v, ln_final_g):
    assert ln_mix_g.shape[0] == 1, "one layer"
    l = 0
    return _layer(x, mem, ln_mix_g[l], w_in[l], conv_w[l], conv_b[l], lru_wa[l], lru_ba[l],
                  lru_wx[l], lru_bx[l], lru_lambda[l], gn_lru_g[l], gn_attn_g[l], w_out[l],
                  ln_cross_g[l], ln_mem_g[l], w_cq[l], w_ck[l], w_cv[l], w_co[l], ln_ffn_g[l],
                  peer_wq[l], peer_subkeys[l], peer_u[l], peer_v[l], ln_final_g,
                  tm=256, ts=512, tt=16)
```

```python
import functools
import math

import jax
import jax.numpy as jnp
from jax import lax
from jax.experimental import pallas as pl
from jax.experimental.pallas import tpu as pltpu

F32 = jnp.float32
BF16 = jnp.bfloat16

EPS = 1e-6
NEG = -1e30

LRU_BLOCKS = 8
CONV_WIDTH = 4
LRU_C = 8.0
ATTN_HEADS = 8
ATTN_HEAD_DIM = 64
MOBA_BLOCK = 256
MOBA_TOPK = 3
CROSS_HEADS = 4
PEER_KEYS = 128
PEER_HEADS = 8
PEER_TOPK = 16
PEER_HALF = 128

LANES = 128
SUBLANES = 8
VMEM_LIMIT = 48 * 1024 * 1024


def _cparams(sem):
    return pltpu.CompilerParams(dimension_semantics=sem, vmem_limit_bytes=VMEM_LIMIT)


def _rms(x, g):
    return x * lax.rsqrt(jnp.mean(x * x, axis=-1, keepdims=True) + EPS) * g


def _gelu(x):
    return 0.5 * x * (1.0 + lax.erf(x * (1.0 / math.sqrt(2.0))))


def _inproj_kernel(x_ref, g_ref, wl_ref, wq_ref, lru_ref, qkv_ref):
    h = _rms(x_ref[...], g_ref[...]).astype(BF16)
    lru_ref[...] = jnp.dot(h, wl_ref[...], preferred_element_type=F32)
    qkv_ref[...] = jnp.dot(h, wq_ref[...], preferred_element_type=F32)


def _inproj(x2, g, w_lru, w_qkv, tm):
    t, d = x2.shape
    nl, nq = w_lru.shape[1], w_qkv.shape[1]
    return pl.pallas_call(
        _inproj_kernel,
        grid=(t // tm,),
        in_specs=[
            pl.BlockSpec((tm, d), lambda i: (i, 0)),
            pl.BlockSpec((1, d), lambda i: (0, 0)),
            pl.BlockSpec((d, nl), lambda i: (0, 0)),
            pl.BlockSpec((d, nq), lambda i: (0, 0)),
        ],
        out_specs=[
            pl.BlockSpec((tm, nl), lambda i: (i, 0)),
            pl.BlockSpec((tm, nq), lambda i: (i, 0)),
        ],
        out_shape=[jax.ShapeDtypeStruct((t, nl), F32), jax.ShapeDtypeStruct((t, nq), F32)],
        compiler_params=_cparams(("parallel",)),
        name="inproj",
    )(x2, g, w_lru, w_qkv)


def _lru_kernel(lru_ref, cw_ref, cb_ref, wa_ref, ba_ref, wx_ref, bx_ref, lam_ref, gn_ref,
                y_ref, xp_ref, a_ref, b_ref, h_ref, hc_ref, *, ts, c):
    ti = pl.program_id(1)

    @pl.when(ti == 0)
    def _():
        xp_ref[0:SUBLANES, :] = jnp.zeros((SUBLANES, c), F32)
        hc_ref[...] = jnp.zeros((SUBLANES, c), F32)

    x = lru_ref[0, :, 0:c]
    gate = lru_ref[0, :, c:2 * c]
    xp_ref[SUBLANES:SUBLANES + ts, :] = x
    xc = cb_ref[...] + cw_ref[CONV_WIDTH - 1:CONV_WIDTH, :] * x
    for k in range(CONV_WIDTH - 1):
        off = SUBLANES - (CONV_WIDTH - 1) + k
        xc = xc + cw_ref[k:k + 1, :] * xp_ref[off:off + ts, :]
    xp_ref[0:SUBLANES, :] = x[ts - SUBLANES:ts, :]

    xb = xc.astype(BF16)
    gate_r = jax.nn.sigmoid(jnp.dot(xb, wa_ref[...], preferred_element_type=F32) + ba_ref[...])
    gate_i = jax.nn.sigmoid(jnp.dot(xb, wx_ref[...], preferred_element_type=F32) + bx_ref[...])
    z = -lam_ref[...]
    softplus = jnp.maximum(z, 0.0) + jnp.log1p(jnp.exp(-jnp.abs(z)))
    log_a = (-LRU_C * gate_r) * softplus
    a = jnp.exp(log_a)
    one_minus_a2 = -jnp.tanh(log_a) * (a * a + 1.0)
    a_ref[...] = a
    b_ref[...] = jnp.sqrt(one_minus_a2) * (gate_i * xc)

    row = lax.broadcasted_iota(jnp.int32, (SUBLANES, c), 0)

    def chunk(i, hprev):
        r0 = pl.multiple_of(i * SUBLANES, SUBLANES)
        a8 = a_ref[pl.ds(r0, SUBLANES), :]
        b8 = b_ref[pl.ds(r0, SUBLANES), :]
        for s in (1, 2, 4):
            keep = row >= s
            a_sh = pltpu.roll(a8, s, axis=0)
            b_sh = pltpu.roll(b8, s, axis=0)
            b8 = jnp.where(keep, b8 + a8 * b_sh, b8)
            a8 = jnp.where(keep, a8 * a_sh, a8)
        h8 = b8 + a8 * hprev
        h_ref[pl.ds(r0, SUBLANES), :] = h8
        return jnp.broadcast_to(h8[SUBLANES - 1:SUBLANES, :], (SUBLANES, c))

    hlast = lax.fori_loop(0, ts // SUBLANES, chunk, hc_ref[...], unroll=4)
    hc_ref[...] = hlast

    y = h_ref[...] * _gelu(gate)
    y_ref[0] = _rms(y, gn_ref[...])


def _lru(lru3, conv_w, conv_b, wa, ba, wx, bx, lam, gn, ts):
    b, s, c2 = lru3.shape
    c = c2 // 2
    vec = lambda: pl.BlockSpec((1, c), lambda i, j: (0, 0))
    mat = lambda: pl.BlockSpec((c, c), lambda i, j: (0, 0))
    return pl.pallas_call(
        functools.partial(_lru_kernel, ts=ts, c=c),
        grid=(b, s // ts),
        in_specs=[
            pl.BlockSpec((1, ts, c2), lambda i, j: (i, j, 0)),
            pl.BlockSpec((CONV_WIDTH, c), lambda i, j: (0, 0)),
            vec(), mat(), vec(), mat(), vec(), vec(), vec(),
        ],
        out_specs=pl.BlockSpec((1, ts, c), lambda i, j: (i, j, 0)),
        out_shape=jax.ShapeDtypeStruct((b, s, c), F32),
        scratch_shapes=[
            pltpu.VMEM((ts + SUBLANES, c), F32),
            pltpu.VMEM((ts, c), F32),
            pltpu.VMEM((ts, c), F32),
            pltpu.VMEM((ts, c), F32),
            pltpu.VMEM((SUBLANES, c), F32),
        ],
        compiler_params=_cparams(("parallel", "arbitrary")),
        name="lru",
    )(lru3, conv_w, conv_b, wa, ba, wx, bx, lam, gn)


def _moba_kernel(q_ref, k_ref, v_ref, o_ref, qb_ref, kb_ref, vb_ref, sel_ref, ex_ref, *, s):
    hh = pl.program_id(2)
    nb = s // MOBA_BLOCK
    lane = lax.broadcasted_iota(jnp.int32, (1, LANES), 1)
    hmask = (lane // ATTN_HEAD_DIM) == hh
    q = jnp.where(hmask, q_ref[0], 0.0)
    k = k_ref[0]
    kb_ref[...] = k.astype(BF16)
    vb_ref[...] = v_ref[0].astype(BF16)
    scale = ATTN_HEAD_DIM ** -0.5
    assert math.frexp(scale)[0] == 0.5, "folding the scale into q is exact only for a power of two"
    qb_ref[...] = (q * scale).astype(BF16)

    kmean = jnp.mean(k.reshape(nb, MOBA_BLOCK, LANES), axis=1)
    gate_t = lax.dot_general(kmean, q, (((1,), (1,)), ((), ())),
                             precision=lax.Precision.HIGHEST, preferred_element_type=F32)
    n_io = lax.broadcasted_iota(jnp.int32, (nb, s), 0)
    qblk = lax.broadcasted_iota(jnp.int32, (nb, s), 1) // MOBA_BLOCK
    rank = jnp.zeros((nb, s), jnp.int32)
    for m in range(nb):
        gm = gate_t[m:m + 1, :]
        beats = (gm > gate_t) | ((gm == gate_t) & (m < n_io))
        rank = rank + jnp.where(beats & (m < qblk), 1, 0)
    sel_t = jnp.where((n_io < qblk) & (rank < MOBA_TOPK), 1.0, 0.0)
    sel_t = jnp.concatenate([sel_t, jnp.zeros((LANES - nb, s), F32)], axis=0)
    sel_ref[...] = sel_t.T.astype(BF16)

    e_n = lax.broadcasted_iota(jnp.int32, (LANES, s), 0)
    e_k = lax.broadcasted_iota(jnp.int32, (LANES, s), 1) // MOBA_BLOCK
    ex_ref[...] = jnp.where(e_n == e_k, 1.0, 0.0).astype(BF16)

    causal = (lax.broadcasted_iota(jnp.int32, (MOBA_BLOCK, MOBA_BLOCK), 1)
              <= lax.broadcasted_iota(jnp.int32, (MOBA_BLOCK, MOBA_BLOCK), 0))
    for j in range(nb):
        nk = (j + 1) * MOBA_BLOCK
        r0 = j * MOBA_BLOCK
        qj = qb_ref[r0:r0 + MOBA_BLOCK, :]
        own = lax.dot_general(qj, kb_ref[r0:nk, :], (((1,), (1,)), ((), ())), preferred_element_type=F32)
        own = jnp.where(causal, own, NEG)
        if j == 0:
            sc = own
        else:
            pst = lax.dot_general(qj, kb_ref[0:r0, :], (((1,), (1,)), ((), ())), preferred_element_type=F32)
            flag = jnp.dot(sel_ref[r0:r0 + MOBA_BLOCK, :], ex_ref[:, 0:r0], preferred_element_type=F32)
            sc = jnp.concatenate([jnp.where(flag > 0.5, pst, NEG), own], axis=1)
        mx = jnp.max(sc, axis=-1, keepdims=True)
        p = jnp.exp(sc - mx)
        l = jnp.sum(p, axis=-1, keepdims=True)
        o = jnp.dot(p.astype(BF16), vb_ref[0:nk, :], preferred_element_type=F32) / l
        o = jnp.where(hmask, o, 0.0)

        @pl.when(hh == 0)
        def _():
            o_ref[0, r0:r0 + MOBA_BLOCK, :] = o

        @pl.when(hh != 0)
        def _():
            o_ref[0, r0:r0 + MOBA_BLOCK, :] += o


def _moba(qkv3, d_attn):
    b, s, _ = qkv3.shape
    npair = d_attn // LANES
    per = LANES // ATTN_HEAD_DIM
    return pl.pallas_call(
        functools.partial(_moba_kernel, s=s),
        grid=(b, npair, per),
        in_specs=[
            pl.BlockSpec((1, s, LANES), lambda i, p, h: (i, 0, p)),
            pl.BlockSpec((1, s, LANES), lambda i, p, h: (i, 0, npair + p)),
            pl.BlockSpec((1, s, LANES), lambda i, p, h: (i, 0, 2 * npair + p)),
        ],
        out_specs=pl.BlockSpec((1, s, LANES), lambda i, p, h: (i, 0, p)),
        out_shape=jax.ShapeDtypeStruct((b, s, d_attn), F32),
        scratch_shapes=[
            pltpu.VMEM((s, LANES), BF16),
            pltpu.VMEM((s, LANES), BF16),
            pltpu.VMEM((s, LANES), BF16),
            pltpu.VMEM((s, LANES), BF16),
            pltpu.VMEM((LANES, s), BF16),
        ],
        compiler_params=_cparams(("parallel", "parallel", "arbitrary")),
        name="moba",
    )(qkv3, qkv3, qkv3)


def _outproj_kernel(x_ref, yl_ref, ya_ref, ga_ref, w1_ref, w2_ref, o_ref):
    ya = _rms(ya_ref[...], ga_ref[...]).astype(BF16)
    yl = yl_ref[...].astype(BF16)
    o_ref[...] = (x_ref[...] + jnp.dot(yl, w1_ref[...], preferred_element_type=F32)
                  + jnp.dot(ya, w2_ref[...], preferred_element_type=F32))


def _outproj(x2, yl, ya, ga, w1, w2, tm):
    t, d = x2.shape
    c = yl.shape[1]
    return pl.pallas_call(
        _outproj_kernel,
        grid=(t // tm,),
        in_specs=[
            pl.BlockSpec((tm, d), lambda i: (i, 0)),
            pl.BlockSpec((tm, c), lambda i: (i, 0)),
            pl.BlockSpec((tm, c), lambda i: (i, 0)),
            pl.BlockSpec((1, c), lambda i: (0, 0)),
            pl.BlockSpec((c, d), lambda i: (0, 0)),
            pl.BlockSpec((c, d), lambda i: (0, 0)),
        ],
        out_specs=pl.BlockSpec((tm, d), lambda i: (i, 0)),
        out_shape=jax.ShapeDtypeStruct((t, d), F32),
        compiler_params=_cparams(("parallel",)),
        name="outproj",
    )(x2, yl, ya, ga, w1, w2)


def _memkv_kernel(m_ref, g_ref, wk_ref, wv_ref, k_ref, v_ref):
    mn = _rms(m_ref[...], g_ref[...]).astype(BF16)
    k_ref[...] = jnp.dot(mn, wk_ref[...], preferred_element_type=F32).astype(BF16)
    v_ref[...] = jnp.dot(mn, wv_ref[...], preferred_element_type=F32).astype(BF16)


def _memkv(mem2, g, wk, wv, tm):
    t, d = mem2.shape
    return pl.pallas_call(
        _memkv_kernel,
        grid=(t // tm,),
        in_specs=[
            pl.BlockSpec((tm, d), lambda i: (i, 0)),
            pl.BlockSpec((1, d), lambda i: (0, 0)),
            pl.BlockSpec((d, d), lambda i: (0, 0)),
            pl.BlockSpec((d, d), lambda i: (0, 0)),
        ],
        out_specs=[pl.BlockSpec((tm, d), lambda i: (i, 0)), pl.BlockSpec((tm, d), lambda i: (i, 0))],
        out_shape=[jax.ShapeDtypeStruct((t, d), BF16), jax.ShapeDtypeStruct((t, d), BF16)],
        compiler_params=_cparams(("parallel",)),
        name="memkv",
    )(mem2, g, wk, wv)


def _cross_kernel(x_ref, g_ref, wq_ref, k_ref, v_ref, wo_ref, o_ref, *, d):
    x = x_ref[0]
    q = jnp.dot(_rms(x, g_ref[...]).astype(BF16), wq_ref[...], preferred_element_type=F32)
    dh = d // CROSS_HEADS
    scale = dh ** -0.5
    k = k_ref[0]
    v = v_ref[0]
    outs = []
    for h in range(CROSS_HEADS):
        qh = q[:, h * dh:(h + 1) * dh].astype(BF16)
        sc = lax.dot_general(qh, k[:, h * dh:(h + 1) * dh], (((1,), (1,)), ((), ())),
                             preferred_element_type=F32) * scale
        mx = jnp.max(sc, axis=-1, keepdims=True)
        p = jnp.exp(sc - mx)
        l = jnp.sum(p, axis=-1, keepdims=True)
        outs.append(jnp.dot(p.astype(BF16), v[:, h * dh:(h + 1) * dh], preferred_element_type=F32) / l)
    o = jnp.concatenate(outs, axis=-1).astype(BF16)
    o_ref[0] = x + jnp.dot(o, wo_ref[...], preferred_element_type=F32)


def _cross(x3, g, wq, mk3, mv3, wo, tm):
    b, s, d = x3.shape
    m = mk3.shape[1]
    return pl.pallas_call(
        functools.partial(_cross_kernel, d=d),
        grid=(b, s // tm),
        in_specs=[
            pl.BlockSpec((1, tm, d), lambda i, j: (i, j, 0)),
            pl.BlockSpec((1, d), lambda i, j: (0, 0)),
            pl.BlockSpec((d, d), lambda i, j: (0, 0)),
            pl.BlockSpec((1, m, d), lambda i, j: (i, 0, 0)),
            pl.BlockSpec((1, m, d), lambda i, j: (i, 0, 0)),
            pl.BlockSpec((d, d), lambda i, j: (0, 0)),
        ],
        out_specs=pl.BlockSpec((1, tm, d), lambda i, j: (i, j, 0)),
        out_shape=jax.ShapeDtypeStruct((b, s, d), F32),
        compiler_params=_cparams(("parallel", "parallel")),
        name="cross",
    )(x3, g, wq, mk3, mv3, wo)


def _topk_rows(sc, kk, nrow):
    n = sc.shape[1]
    rows = lax.broadcasted_iota(jnp.int32, (nrow, n), 0)
    vals, idxs = [], []
    for _ in range(kk):
        m = jnp.max(sc, axis=0, keepdims=True)
        idx = jnp.min(jnp.where(sc == m, rows, nrow), axis=0, keepdims=True)
        vals.append(m)
        idxs.append(idx)
        sc = jnp.where(rows == idx, -jnp.inf, sc)
    return jnp.concatenate(vals, axis=0), jnp.concatenate(idxs, axis=0)


def _route_kernel(x_ref, g_ref, wq_ref, sk_ref, xn_ref, ids_ref, gates_ref, *, tm):
    xn = _rms(x_ref[...], g_ref[...])
    xn_ref[...] = xn
    pq = jnp.dot(xn.astype(BF16), wq_ref[...], preferred_element_type=F32)
    kk = PEER_TOPK
    assert kk == 2 * SUBLANES, "candidate layout below is written for 16 = 2 x 8 sublanes"
    ids_all, gates_all = [], []
    for h in range(PEER_HEADS):
        tops = []
        for p in range(2):
            c0 = (h * 2 + p) * PEER_HALF
            qhp = pq[:, c0:c0 + PEER_HALF]
            st = lax.dot_general(sk_ref[h * 2 + p], qhp, (((1,), (1,)), ((), ())),
                                 precision=lax.Precision.HIGHEST, preferred_element_type=F32)
            tops.append(_topk_rows(st, kk, PEER_KEYS))
        (s0, i0), (s1, i1) = tops
        sub = lax.broadcasted_iota(jnp.int32, (SUBLANES, tm), 0)
        ps, pi = [s0[0:1, :] + s1], [i0[0:1, :] * PEER_KEYS + i1]
        for a in range(1, SUBLANES):
            nb = kk // (a + 1)
            ps.append(jnp.where(sub < nb, s0[a:a + 1, :] + s1[0:SUBLANES, :], -jnp.inf))
            pi.append(i0[a:a + 1, :] * PEER_KEYS + i1[0:SUBLANES, :])
        ps.append(s0[SUBLANES:kk, :] + s1[0:1, :])
        pi.append(i0[SUBLANES:kk, :] * PEER_KEYS + i1[0:1, :])
        cand_s = jnp.concatenate(ps, axis=0)
        cand_i = jnp.concatenate(pi, axis=0)
        ncand = cand_s.shape[0]
        rows = lax.broadcasted_iota(jnp.int32, (ncand, tm), 0)
        best, ids = [], []
        for _ in range(kk):
            m = jnp.max(cand_s, axis=0, keepdims=True)
            pos = jnp.min(jnp.where(cand_s == m, rows, ncand), axis=0, keepdims=True)
            hit = rows == pos
            best.append(m)
            ids.append(jnp.max(jnp.where(hit, cand_i, -1), axis=0, keepdims=True))
            cand_s = jnp.where(hit, -jnp.inf, cand_s)
        best = jnp.concatenate(best, axis=0)
        e = jnp.exp(best - best[0:1, :])
        gates_all.append(e / jnp.sum(e, axis=0, keepdims=True))
        ids_all.append(jnp.concatenate(ids, axis=0))
    ids_ref[...] = jnp.concatenate(ids_all, axis=0).T
    gates_ref[...] = jnp.concatenate(gates_all, axis=0).T


def _route(x2, g, wq, subkeys, tm):
    t, d = x2.shape
    nq = wq.shape[1]
    ne = PEER_HEADS * PEER_TOPK
    return pl.pallas_call(
        functools.partial(_route_kernel, tm=tm),
        grid=(t // tm,),
        in_specs=[
            pl.BlockSpec((tm, d), lambda i: (i, 0)),
            pl.BlockSpec((1, d), lambda i: (0, 0)),
            pl.BlockSpec((d, nq), lambda i: (0, 0)),
            pl.BlockSpec((PEER_HEADS * 2, PEER_KEYS, PEER_HALF), lambda i: (0, 0, 0)),
        ],
        out_specs=[
            pl.BlockSpec((tm, d), lambda i: (i, 0)),
            pl.BlockSpec((tm, ne), lambda i: (i, 0)),
            pl.BlockSpec((tm, ne), lambda i: (i, 0)),
        ],
        out_shape=[
            jax.ShapeDtypeStruct((t, d), F32),
            jax.ShapeDtypeStruct((t, ne), jnp.int32),
            jax.ShapeDtypeStruct((t, ne), F32),
        ],
        compiler_params=_cparams(("parallel",)),
        name="route",
    )(x2, g, wq, subkeys)


VALUE_LAG = 2
ROW_PITCH = 20


def _experts_kernel(ids_ref, idn_ref, x_ref, xn_ref, gates_ref, gf_ref, tab_ref, o_ref,
                    buf_ref, sem_ref, *, tt, d, ne, nsteps):
    i = pl.program_id(0)
    slot = lax.rem(i, 2)
    nch = d // LANES
    nrow = 2 * nch
    per = ne // nrow
    tok_rows = ne * ROW_PITCH
    slot_rows = tt * tok_rows
    contract_last = (((1,), (1,)), ((), ()))

    def issue(idr, sl, t, e0, e1):
        for e in range(e0, e1):
            src0 = pl.multiple_of(idr[t * ne + e] * nrow, nrow)
            dst0 = pl.multiple_of(sl * slot_rows + t * tok_rows + e * ROW_PITCH, 4)
            pltpu.make_async_copy(tab_ref.at[pl.ds(src0, nrow), :], buf_ref.at[pl.ds(dst0, nrow), :],
                                  sem_ref.at[sl]).start(priority=e % 2)

    def wait_slot(sl):
        n = tt * ne * nrow
        dst0 = pl.multiple_of(sl * slot_rows, nrow)
        pltpu.make_async_copy(tab_ref.at[pl.ds(0, n), :], buf_ref.at[pl.ds(dst0, n), :],
                              sem_ref.at[sl]).wait()

    def chunk(t, r):
        base = slot * slot_rows + t * tok_rows
        return buf_ref[pl.ds(base + r, ne, stride=ROW_PITCH), :].astype(BF16)

    @pl.when(i == 0)
    def _():
        def first(t, carry):
            issue(ids_ref, 0, t, 0, ne)
            return carry
        lax.fori_loop(0, tt, first, 0)

    wait_slot(slot)

    xg = xn_ref[...]
    gg = gates_ref[...]
    todo = iter([(t, e) for t in range(tt) for e in range(ne)])

    def issue_next():
        for _ in range(per):
            t, e = next(todo)
            issue(idn_ref, 1 - slot, t, e, e + 1)

    def value_row(t, act):
        w = (gg[t:t + 1, :] * _gelu(act)).astype(BF16)
        pieces = []
        for c in range(nch):
            issue_next()
            pieces.append(jnp.dot(w, chunk(t, nch + c), preferred_element_type=F32)[0:1, :])
        return jnp.concatenate(pieces, axis=1)

    ys = []
    acts = {}
    for t in range(tt):
        act = None
        for c in range(nch):
            issue_next()
            xc = jnp.broadcast_to(xg[t:t + 1, c * LANES:(c + 1) * LANES], (SUBLANES, LANES))
            part = lax.dot_general(xc.astype(BF16), chunk(t, c), contract_last,
                                   preferred_element_type=F32)
            act = part if act is None else act + part
        acts[t] = act
        if t >= VALUE_LAG:
            ys.append(value_row(t - VALUE_LAG, acts.pop(t - VALUE_LAG)))
    for t in sorted(acts):
        ys.append(value_row(t, acts[t]))
    o_ref[...] = _rms(x_ref[...] + jnp.concatenate(ys, axis=0), gf_ref[...])

    @pl.when(i == nsteps - 1)
    def _():
        wait_slot(1 - slot)


def _stack_uv(u, v):
    n, d = u.shape
    nch = d // LANES
    uv = jnp.concatenate([u.reshape(n, nch, LANES), v.reshape(n, nch, LANES)], axis=1)
    return uv.reshape(n * 2 * nch, LANES)


def _experts(ids, x2, xn, gates, gf, table, tt):
    t, d = x2.shape
    ne = ids.shape[1]
    nsteps = t // tt
    ids = ids.reshape(t * ne)
    return pl.pallas_call(
        functools.partial(_experts_kernel, tt=tt, d=d, ne=ne, nsteps=nsteps),
        grid=(nsteps,),
        in_specs=[
            pl.BlockSpec((tt * ne,), lambda i: (i,), memory_space=pltpu.SMEM),
            pl.BlockSpec((tt * ne,), lambda i: (jnp.minimum(i + 1, nsteps - 1),), memory_space=pltpu.SMEM),
            pl.BlockSpec((tt, d), lambda i: (i, 0)),
            pl.BlockSpec((tt, d), lambda i: (i, 0)),
            pl.BlockSpec((tt, ne), lambda i: (i, 0)),
            pl.BlockSpec((1, d), lambda i: (0, 0)),
            pl.BlockSpec(memory_space=pl.ANY),
        ],
        out_specs=pl.BlockSpec((tt, d), lambda i: (i, 0)),
        out_shape=jax.ShapeDtypeStruct((t, d), F32),
        scratch_shapes=[
            pltpu.VMEM((2 * tt * ne * ROW_PITCH, LANES), F32),
            pltpu.SemaphoreType.DMA((2,)),
        ],
        compiler_params=_cparams(("arbitrary",)),
        name="experts",
    )(ids, ids, x2, xn, gates, gf, table)


def _block_diag(w):
    n, a, b = w.shape
    eye = jnp.eye(n, dtype=w.dtype)
    return (eye[:, None, :, None] * w[:, :, None, :]).reshape(n * a, n * b)


def _layer(x, mem, ln_mix_g, w_in, conv_w, conv_b, lru_wa, lru_ba, lru_wx, lru_bx, lru_lambda,
           gn_lru_g, gn_attn_g, w_out, ln_cross_g, ln_mem_g, w_cq, w_ck, w_cv, w_co, ln_ffn_g,
           peer_wq, peer_subkeys, peer_u, peer_v, ln_final_g, *, tm, ts, tt):
    b, s, d = x.shape
    t = b * s
    c = d // 2
    row = lambda a: a.reshape(1, -1)
    x2 = x.reshape(t, d)

    w_in_b = w_in.astype(BF16)
    lru2, qkv2 = _inproj(x2, row(ln_mix_g), w_in_b[:, :2 * c], w_in_b[:, 2 * c:], tm)

    y_lru = _lru(lru2.reshape(b, s, 2 * c), conv_w, row(conv_b),
                 _block_diag(lru_wa).astype(BF16), row(lru_ba),
                 _block_diag(lru_wx).astype(BF16), row(lru_bx),
                 row(lru_lambda), row(gn_lru_g), ts)
    y_attn = _moba(qkv2.reshape(b, s, 3 * c), c)

    w_out_b = w_out.astype(BF16)
    x1 = _outproj(x2, y_lru.reshape(t, c), y_attn.reshape(t, c), row(gn_attn_g),
                  w_out_b[:c], w_out_b[c:], tm)

    m = mem.shape[1]
    mk, mv = _memkv(mem.reshape(b * m, d), row(ln_mem_g), w_ck.astype(BF16), w_cv.astype(BF16), m)
    x2b = _cross(x1.reshape(b, s, d), row(ln_cross_g), w_cq.astype(BF16),
                 mk.reshape(b, m, d), mv.reshape(b, m, d), w_co.astype(BF16), tm)

    xr = x2b.reshape(t, d)
    xn, ids, gates = _route(xr, row(ln_ffn_g), peer_wq.astype(BF16),
                            peer_subkeys.reshape(PEER_HEADS * 2, PEER_KEYS, PEER_HALF), tm)
    out = _experts(ids, xr, xn, gates, row(ln_final_g), _stack_uv(peer_u, peer_v), tt)
    return out.reshape(b, s, d)


def kernel(x, mem, ln_mix_g, w_in, conv_w, conv_b, lru_wa, lru_ba, lru_wx, lru_bx, lru_lambda, gn_lru_g, gn_attn_g, w_out, ln_cross_g, ln_mem_g, w_cq, w_ck, w_cv, w_co, ln_ffn_g, peer_wq, peer_subkeys, peer_u, peer_v, ln_final_g):
    assert ln_mix_g.shape[0] == 1, "one layer"
    l = 0
    return _layer(x, mem, ln_mix_g[l], w_in[l], conv_w[l], conv_b[l], lru_wa[l], lru_ba[l],
                  lru_wx[l], lru_bx[l], lru_lambda[l], gn_lru_g[l], gn_attn_g[l], w_out[l],
                  ln_cross_g[l], ln_mem_g[l], w_cq[l], w_ck[l], w_cv[l], w_co[l], ln_ffn_g[l],
                  peer_wq[l], peer_subkeys[l], peer_u[l], peer_v[l], ln_final_g,
                  tm=256, ts=512, tt=16)
```

```python
import functools
import math

import jax
import jax.numpy as jnp
from jax import lax
from jax.experimental import pallas as pl
from jax.experimental.pallas import tpu as pltpu

F32 = jnp.float32
BF16 = jnp.bfloat16

EPS = 1e-6
NEG = -1e30

LRU_BLOCKS = 8
CONV_WIDTH = 4
LRU_C = 8.0
ATTN_HEADS = 8
ATTN_HEAD_DIM = 64
MOBA_BLOCK = 256
MOBA_TOPK = 3
CROSS_HEADS = 4
PEER_KEYS = 128
PEER_HEADS = 8
PEER_TOPK = 16
PEER_HALF = 128

LANES = 128
SUBLANES = 8
VMEM_LIMIT = 48 * 1024 * 1024


def _cparams(sem):
    return pltpu.CompilerParams(dimension_semantics=sem, vmem_limit_bytes=VMEM_LIMIT)


def _rms(x, g):
    return x * lax.rsqrt(jnp.mean(x * x, axis=-1, keepdims=True) + EPS) * g


def _gelu(x):
    return 0.5 * x * (1.0 + lax.erf(x * (1.0 / math.sqrt(2.0))))


def _inproj_kernel(x_ref, g_ref, wl_ref, wq_ref, lru_ref, qkv_ref):
    h = _rms(x_ref[...], g_ref[...]).astype(BF16)
    lru_ref[...] = jnp.dot(h, wl_ref[...], preferred_element_type=F32)
    qkv_ref[...] = jnp.dot(h, wq_ref[...], preferred_element_type=F32)


def _inproj(x2, g, w_lru, w_qkv, tm):
    t, d = x2.shape
    nl, nq = w_lru.shape[1], w_qkv.shape[1]
    return pl.pallas_call(
        _inproj_kernel,
        grid=(t // tm,),
        in_specs=[
            pl.BlockSpec((tm, d), lambda i: (i, 0)),
            pl.BlockSpec((1, d), lambda i: (0, 0)),
            pl.BlockSpec((d, nl), lambda i: (0, 0)),
            pl.BlockSpec((d, nq), lambda i: (0, 0)),
        ],
        out_specs=[
            pl.BlockSpec((tm, nl), lambda i: (i, 0)),
            pl.BlockSpec((tm, nq), lambda i: (i, 0)),
        ],
        out_shape=[jax.ShapeDtypeStruct((t, nl), F32), jax.ShapeDtypeStruct((t, nq), F32)],
        compiler_params=_cparams(("parallel",)),
        name="inproj",
    )(x2, g, w_lru, w_qkv)


def _lru_kernel(lru_ref, cw_ref, cb_ref, wa_ref, ba_ref, wx_ref, bx_ref, lam_ref, gn_ref,
                y_ref, xp_ref, a_ref, b_ref, h_ref, hc_ref, *, ts, c):
    ti = pl.program_id(1)

    @pl.when(ti == 0)
    def _():
        xp_ref[0:SUBLANES, :] = jnp.zeros((SUBLANES, c), F32)
        hc_ref[...] = jnp.zeros((SUBLANES, c), F32)

    x = lru_ref[0, :, 0:c]
    gate = lru_ref[0, :, c:2 * c]
    xp_ref[SUBLANES:SUBLANES + ts, :] = x
    xc = cb_ref[...] + cw_ref[CONV_WIDTH - 1:CONV_WIDTH, :] * x
    for k in range(CONV_WIDTH - 1):
        off = SUBLANES - (CONV_WIDTH - 1) + k
        xc = xc + cw_ref[k:k + 1, :] * xp_ref[off:off + ts, :]
    xp_ref[0:SUBLANES, :] = x[ts - SUBLANES:ts, :]

    xb = xc.astype(BF16)
    gate_r = jax.nn.sigmoid(jnp.dot(xb, wa_ref[...], preferred_element_type=F32) + ba_ref[...])
    gate_i = jax.nn.sigmoid(jnp.dot(xb, wx_ref[...], preferred_element_type=F32) + bx_ref[...])
    z = -lam_ref[...]
    softplus = jnp.maximum(z, 0.0) + jnp.log1p(jnp.exp(-jnp.abs(z)))
    log_a = (-LRU_C * gate_r) * softplus
    a = jnp.exp(log_a)
    one_minus_a2 = -jnp.tanh(log_a) * (a * a + 1.0)
    a_ref[...] = a
    b_ref[...] = jnp.sqrt(one_minus_a2) * (gate_i * xc)

    row = lax.broadcasted_iota(jnp.int32, (SUBLANES, c), 0)

    def chunk(i, hprev):
        r0 = pl.multiple_of(i * SUBLANES, SUBLANES)
        a8 = a_ref[pl.ds(r0, SUBLANES), :]
        b8 = b_ref[pl.ds(r0, SUBLANES), :]
        for s in (1, 2, 4):
            keep = row >= s
            a_sh = pltpu.roll(a8, s, axis=0)
            b_sh = pltpu.roll(b8, s, axis=0)
            b8 = jnp.where(keep, b8 + a8 * b_sh, b8)
            a8 = jnp.where(keep, a8 * a_sh, a8)
        h8 = b8 + a8 * hprev
        h_ref[pl.ds(r0, SUBLANES), :] = h8
        return jnp.broadcast_to(h8[SUBLANES - 1:SUBLANES, :], (SUBLANES, c))

    hlast = lax.fori_loop(0, ts // SUBLANES, chunk, hc_ref[...], unroll=4)
    hc_ref[...] = hlast

    y = h_ref[...] * _gelu(gate)
    y_ref[0] = _rms(y, gn_ref[...])


def _lru(lru3, conv_w, conv_b, wa, ba, wx, bx, lam, gn, ts):
    b, s, c2 = lru3.shape
    c = c2 // 2
    vec = lambda: pl.BlockSpec((1, c), lambda i, j: (0, 0))
    mat = lambda: pl.BlockSpec((c, c), lambda i, j: (0, 0))
    return pl.pallas_call(
        functools.partial(_lru_kernel, ts=ts, c=c),
        grid=(b, s // ts),
        in_specs=[
            pl.BlockSpec((1, ts, c2), lambda i, j: (i, j, 0)),
            pl.BlockSpec((CONV_WIDTH, c), lambda i, j: (0, 0)),
            vec(), mat(), vec(), mat(), vec(), vec(), vec(),
        ],
        out_specs=pl.BlockSpec((1, ts, c), lambda i, j: (i, j, 0)),
        out_shape=jax.ShapeDtypeStruct((b, s, c), F32),
        scratch_shapes=[
            pltpu.VMEM((ts + SUBLANES, c), F32),
            pltpu.VMEM((ts, c), F32),
            pltpu.VMEM((ts, c), F32),
            pltpu.VMEM((ts, c), F32),
            pltpu.VMEM((SUBLANES, c), F32),
        ],
        compiler_params=_cparams(("parallel", "arbitrary")),
        name="lru",
    )(lru3, conv_w, conv_b, wa, ba, wx, bx, lam, gn)


def _moba_kernel(q_ref, k_ref, v_ref, o_ref, qb_ref, kb_ref, vb_ref, sel_ref, ex_ref, *, s):
    hh = pl.program_id(2)
    nb = s // MOBA_BLOCK
    lane = lax.broadcasted_iota(jnp.int32, (1, LANES), 1)
    hmask = (lane // ATTN_HEAD_DIM) == hh
    q = jnp.where(hmask, q_ref[0], 0.0)
    k = k_ref[0]
    kb_ref[...] = k.astype(BF16)
    vb_ref[...] = v_ref[0].astype(BF16)
    scale = ATTN_HEAD_DIM ** -0.5
    assert math.frexp(scale)[0] == 0.5, "folding the scale into q is exact only for a power of two"
    qb_ref[...] = (q * scale).astype(BF16)

    kmean = jnp.mean(k.reshape(nb, MOBA_BLOCK, LANES), axis=1)
    gate_t = lax.dot_general(kmean, q, (((1,), (1,)), ((), ())),
                             precision=lax.Precision.HIGHEST, preferred_element_type=F32)
    n_io = lax.broadcasted_iota(jnp.int32, (nb, s), 0)
    qblk = lax.broadcasted_iota(jnp.int32, (nb, s), 1) // MOBA_BLOCK
    rank = jnp.zeros((nb, s), jnp.int32)
    for m in range(nb):
        gm = gate_t[m:m + 1, :]
        beats = (gm > gate_t) | ((gm == gate_t) & (m < n_io))
        rank = rank + jnp.where(beats & (m < qblk), 1, 0)
    sel_t = jnp.where((n_io < qblk) & (rank < MOBA_TOPK), 1.0, 0.0)
    sel_t = jnp.concatenate([sel_t, jnp.zeros((LANES - nb, s), F32)], axis=0)
    sel_ref[...] = sel_t.T.astype(BF16)

    e_n = lax.broadcasted_iota(jnp.int32, (LANES, s), 0)
    e_k = lax.broadcasted_iota(jnp.int32, (LANES, s), 1) // MOBA_BLOCK
    ex_ref[...] = jnp.where(e_n == e_k, 1.0, 0.0).astype(BF16)

    causal = (lax.broadcasted_iota(jnp.int32, (MOBA_BLOCK, MOBA_BLOCK), 1)
              <= lax.broadcasted_iota(jnp.int32, (MOBA_BLOCK, MOBA_BLOCK), 0))
    for j in range(nb):
        nk = (j + 1) * MOBA_BLOCK
        r0 = j * MOBA_BLOCK
        qj = qb_ref[r0:r0 + MOBA_BLOCK, :]
        own = lax.dot_general(qj, kb_ref[r0:nk, :], (((1,), (1,)), ((), ())), preferred_element_type=F32)
        own = jnp.where(causal, own, NEG)
        if j == 0:
            sc = own
        else:
            pst = lax.dot_general(qj, kb_ref[0:r0, :], (((1,), (1,)), ((), ())), preferred_element_type=F32)
            flag = jnp.dot(sel_ref[r0:r0 + MOBA_BLOCK, :], ex_ref[:, 0:r0], preferred_element_type=F32)
            sc = jnp.concatenate([jnp.where(flag > 0.5, pst, NEG), own], axis=1)
        mx = jnp.max(sc, axis=-1, keepdims=True)
        p = jnp.exp(sc - mx)
        l = jnp.sum(p, axis=-1, keepdims=True)
        o = jnp.dot(p.astype(BF16), vb_ref[0:nk, :], preferred_element_type=F32) / l
        o = jnp.where(hmask, o, 0.0)

        @pl.when(hh == 0)
        def _():
            o_ref[0, r0:r0 + MOBA_BLOCK, :] = o

        @pl.when(hh != 0)
        def _():
            o_ref[0, r0:r0 + MOBA_BLOCK, :] += o


def _moba(qkv3, d_attn):
    b, s, _ = qkv3.shape
    npair = d_attn // LANES
    per = LANES // ATTN_HEAD_DIM
    return pl.pallas_call(
        functools.partial(_moba_kernel, s=s),
        grid=(b, npair, per),
        in_specs=[
            pl.BlockSpec((1, s, LANES), lambda i, p, h: (i, 0, p)),
            pl.BlockSpec((1, s, LANES), lambda i, p, h: (i, 0, npair + p)),
            pl.BlockSpec((1, s, LANES), lambda i, p, h: (i, 0, 2 * npair + p)),
        ],
        out_specs=pl.BlockSpec((1, s, LANES), lambda i, p, h: (i, 0, p)),
        out_shape=jax.ShapeDtypeStruct((b, s, d_attn), F32),
        scratch_shapes=[
            pltpu.VMEM((s, LANES), BF16),
            pltpu.VMEM((s, LANES), BF16),
            pltpu.VMEM((s, LANES), BF16),
            pltpu.VMEM((s, LANES), BF16),
            pltpu.VMEM((LANES, s), BF16),
        ],
        compiler_params=_cparams(("parallel", "parallel", "arbitrary")),
        name="moba",
    )(qkv3, qkv3, qkv3)


def _outproj_kernel(x_ref, yl_ref, ya_ref, ga_ref, w1_ref, w2_ref, o_ref):
    ya = _rms(ya_ref[...], ga_ref[...]).astype(BF16)
    yl = yl_ref[...].astype(BF16)
    o_ref[...] = (x_ref[...] + jnp.dot(yl, w1_ref[...], preferred_element_type=F32)
                  + jnp.dot(ya, w2_ref[...], preferred_element_type=F32))


def _outproj(x2, yl, ya, ga, w1, w2, tm):
    t, d = x2.shape
    c = yl.shape[1]
    return pl.pallas_call(
        _outproj_kernel,
        grid=(t // tm,),
        in_specs=[
            pl.BlockSpec((tm, d), lambda i: (i, 0)),
            pl.BlockSpec((tm, c), lambda i: (i, 0)),
            pl.BlockSpec((tm, c), lambda i: (i, 0)),
            pl.BlockSpec((1, c), lambda i: (0, 0)),
            pl.BlockSpec((c, d), lambda i: (0, 0)),
            pl.BlockSpec((c, d), lambda i: (0, 0)),
        ],
        out_specs=pl.BlockSpec((tm, d), lambda i: (i, 0)),
        out_shape=jax.ShapeDtypeStruct((t, d), F32),
        compiler_params=_cparams(("parallel",)),
        name="outproj",
    )(x2, yl, ya, ga, w1, w2)


def _memkv_kernel(m_ref, g_ref, wk_ref, wv_ref, k_ref, v_ref):
    mn = _rms(m_ref[...], g_ref[...]).astype(BF16)
    k_ref[...] = jnp.dot(mn, wk_ref[...], preferred_element_type=F32).astype(BF16)
    v_ref[...] = jnp.dot(mn, wv_ref[...], preferred_element_type=F32).astype(BF16)


def _memkv(mem2, g, wk, wv, tm):
    t, d = mem2.shape
    return pl.pallas_call(
        _memkv_kernel,
        grid=(t // tm,),
        in_specs=[
            pl.BlockSpec((tm, d), lambda i: (i, 0)),
            pl.BlockSpec((1, d), lambda i: (0, 0)),
            pl.BlockSpec((d, d), lambda i: (0, 0)),
            pl.BlockSpec((d, d), lambda i: (0, 0)),
        ],
        out_specs=[pl.BlockSpec((tm, d), lambda i: (i, 0)), pl.BlockSpec((tm, d), lambda i: (i, 0))],
        out_shape=[jax.ShapeDtypeStruct((t, d), BF16), jax.ShapeDtypeStruct((t, d), BF16)],
        compiler_params=_cparams(("parallel",)),
        name="memkv",
    )(mem2, g, wk, wv)


def _cross_kernel(x_ref, g_ref, wq_ref, k_ref, v_ref, wo_ref, o_ref, *, d):
    x = x_ref[0]
    q = jnp.dot(_rms(x, g_ref[...]).astype(BF16), wq_ref[...], preferred_element_type=F32)
    dh = d // CROSS_HEADS
    scale = dh ** -0.5
    k = k_ref[0]
    v = v_ref[0]
    outs = []
    for h in range(CROSS_HEADS):
        qh = q[:, h * dh:(h + 1) * dh].astype(BF16)
        sc = lax.dot_general(qh, k[:, h * dh:(h + 1) * dh], (((1,), (1,)), ((), ())),
                             preferred_element_type=F32) * scale
        mx = jnp.max(sc, axis=-1, keepdims=True)
        p = jnp.exp(sc - mx)
        l = jnp.sum(p, axis=-1, keepdims=True)
        outs.append(jnp.dot(p.astype(BF16), v[:, h * dh:(h + 1) * dh], preferred_element_type=F32) / l)
    o = jnp.concatenate(outs, axis=-1).astype(BF16)
    o_ref[0] = x + jnp.dot(o, wo_ref[...], preferred_element_type=F32)


def _cross(x3, g, wq, mk3, mv3, wo, tm):
    b, s, d = x3.shape
    m = mk3.shape[1]
    return pl.pallas_call(
        functools.partial(_cross_kernel, d=d),
        grid=(b, s // tm),
        in_specs=[
            pl.BlockSpec((1, tm, d), lambda i, j: (i, j, 0)),
            pl.BlockSpec((1, d), lambda i, j: (0, 0)),
            pl.BlockSpec((d, d), lambda i, j: (0, 0)),
            pl.BlockSpec((1, m, d), lambda i, j: (i, 0, 0)),
            pl.BlockSpec((1, m, d), lambda i, j: (i, 0, 0)),
            pl.BlockSpec((d, d), lambda i, j: (0, 0)),
        ],
        out_specs=pl.BlockSpec((1, tm, d), lambda i, j: (i, j, 0)),
        out_shape=jax.ShapeDtypeStruct((b, s, d), F32),
        compiler_params=_cparams(("parallel", "parallel")),
        name="cross",
    )(x3, g, wq, mk3, mv3, wo)


def _row_index(nrow, n):
    return lax.broadcasted_iota(jnp.int32, (nrow, n), 0).astype(F32)


def _topk_rows(sc, kk, rows):
    nrow = float(sc.shape[0])
    vals, idxs = [], []
    for _ in range(kk):
        m = jnp.max(sc, axis=0, keepdims=True)
        idx = jnp.min(jnp.where(sc == m, rows, nrow), axis=0, keepdims=True)
        vals.append(m)
        idxs.append(idx)
        sc = jnp.where(rows == idx, -jnp.inf, sc)
    return jnp.concatenate(vals, axis=0), jnp.concatenate(idxs, axis=0)


def _route_kernel(x_ref, g_ref, wq_ref, sk_ref, xn_ref, ids_ref, gates_ref, *, tm):
    xn = _rms(x_ref[...], g_ref[...])
    xn_ref[...] = xn
    pq = jnp.dot(xn.astype(BF16), wq_ref[...], preferred_element_type=F32)
    kk = PEER_TOPK
    assert kk == 2 * SUBLANES, "candidate layout below is written for 16 = 2 x 8 sublanes"
    ids_all, gates_all = [], []
    key_rows = _row_index(PEER_KEYS, tm)
    for h in range(PEER_HEADS):
        tops = []
        for p in range(2):
            c0 = (h * 2 + p) * PEER_HALF
            qhp = pq[:, c0:c0 + PEER_HALF]
            st = lax.dot_general(sk_ref[h * 2 + p], qhp, (((1,), (1,)), ((), ())),
                                 precision=lax.Precision.HIGHEST, preferred_element_type=F32)
            tops.append(_topk_rows(st, kk, key_rows))
        (s0, i0), (s1, i1) = tops
        sub = lax.broadcasted_iota(jnp.int32, (SUBLANES, tm), 0)
        ps, pi = [s0[0:1, :] + s1], [i0[0:1, :] * PEER_KEYS + i1]
        for a in range(1, SUBLANES):
            nb = kk // (a + 1)
            ps.append(jnp.where(sub < nb, s0[a:a + 1, :] + s1[0:SUBLANES, :], -jnp.inf))
            pi.append(i0[a:a + 1, :] * PEER_KEYS + i1[0:SUBLANES, :])
        ps.append(s0[SUBLANES:kk, :] + s1[0:1, :])
        pi.append(i0[SUBLANES:kk, :] * PEER_KEYS + i1[0:1, :])
        cand_s = jnp.concatenate(ps, axis=0)
        cand_i = jnp.concatenate(pi, axis=0)
        ncand = cand_s.shape[0]
        rows = _row_index(ncand, tm)
        best, ids = [], []
        for _ in range(kk):
            m = jnp.max(cand_s, axis=0, keepdims=True)
            pos = jnp.min(jnp.where(cand_s == m, rows, float(ncand)), axis=0, keepdims=True)
            hit = rows == pos
            best.append(m)
            ids.append(jnp.max(jnp.where(hit, cand_i, -1.0), axis=0, keepdims=True))
            cand_s = jnp.where(hit, -jnp.inf, cand_s)
        best = jnp.concatenate(best, axis=0)
        e = jnp.exp(best - best[0:1, :])
        gates_all.append(e / jnp.sum(e, axis=0, keepdims=True))
        ids_all.append(jnp.concatenate(ids, axis=0))
    ids_ref[...] = jnp.concatenate(ids_all, axis=0).T.astype(jnp.int32)
    gates_ref[...] = jnp.concatenate(gates_all, axis=0).T


def _route(x2, g, wq, subkeys, tm):
    t, d = x2.shape
    nq = wq.shape[1]
    ne = PEER_HEADS * PEER_TOPK
    return pl.pallas_call(
        functools.partial(_route_kernel, tm=tm),
        grid=(t // tm,),
        in_specs=[
            pl.BlockSpec((tm, d), lambda i: (i, 0)),
            pl.BlockSpec((1, d), lambda i: (0, 0)),
            pl.BlockSpec((d, nq), lambda i: (0, 0)),
            pl.BlockSpec((PEER_HEADS * 2, PEER_KEYS, PEER_HALF), lambda i: (0, 0, 0)),
        ],
        out_specs=[
            pl.BlockSpec((tm, d), lambda i: (i, 0)),
            pl.BlockSpec((tm, ne), lambda i: (i, 0)),
            pl.BlockSpec((tm, ne), lambda i: (i, 0)),
        ],
        out_shape=[
            jax.ShapeDtypeStruct((t, d), F32),
            jax.ShapeDtypeStruct((t, ne), jnp.int32),
            jax.ShapeDtypeStruct((t, ne), F32),
        ],
        compiler_params=_cparams(("parallel",)),
        name="route",
    )(x2, g, wq, subkeys)


VALUE_LAG = 2
ROW_PITCH = 20


def _experts_kernel(ids_ref, idn_ref, x_ref, xn_ref, gates_ref, gf_ref, tab_ref, o_ref,
                    buf_ref, sem_ref, *, tt, d, ne, nsteps):
    i = pl.program_id(0)
    slot = lax.rem(i, 2)
    nch = d // LANES
    nrow = 2 * nch
    per = ne // nrow
    tok_rows = ne * ROW_PITCH
    slot_rows = tt * tok_rows
    contract_last = (((1,), (1,)), ((), ()))

    def issue(idr, sl, t, e0, e1):
        for e in range(e0, e1):
            src0 = pl.multiple_of(idr[t * ne + e] * nrow, nrow)
            dst0 = pl.multiple_of(sl * slot_rows + t * tok_rows + e * ROW_PITCH, 4)
            pltpu.make_async_copy(tab_ref.at[pl.ds(src0, nrow), :], buf_ref.at[pl.ds(dst0, nrow), :],
                                  sem_ref.at[sl]).start(priority=e % 2)

    def wait_slot(sl):
        n = tt * ne * nrow
        dst0 = pl.multiple_of(sl * slot_rows, nrow)
        pltpu.make_async_copy(tab_ref.at[pl.ds(0, n), :], buf_ref.at[pl.ds(dst0, n), :],
                              sem_ref.at[sl]).wait()

    def chunk(t, r):
        base = slot * slot_rows + t * tok_rows
        return buf_ref[pl.ds(base + r, ne, stride=ROW_PITCH), :].astype(BF16)

    @pl.when(i == 0)
    def _():
        def first(t, carry):
            issue(ids_ref, 0, t, 0, ne)
            return carry
        lax.fori_loop(0, tt, first, 0)

    wait_slot(slot)

    xg = xn_ref[...]
    gg = gates_ref[...]
    todo = iter([(t, e) for t in range(tt) for e in range(ne)])

    def issue_next():
        for _ in range(per):
            t, e = next(todo)
            issue(idn_ref, 1 - slot, t, e, e + 1)

    def value_row(t, act):
        w = (gg[t:t + 1, :] * _gelu(act)).astype(BF16)
        pieces = []
        for c in range(nch):
            issue_next()
            pieces.append(jnp.dot(w, chunk(t, nch + c), preferred_element_type=F32)[0:1, :])
        return jnp.concatenate(pieces, axis=1)

    ys = []
    acts = {}
    for t in range(tt):
        act = None
        for c in range(nch):
            issue_next()
            xc = jnp.broadcast_to(xg[t:t + 1, c * LANES:(c + 1) * LANES], (SUBLANES, LANES))
            part = lax.dot_general(xc.astype(BF16), chunk(t, c), contract_last,
                                   preferred_element_type=F32)
            act = part if act is None else act + part
        acts[t] = act
        if t >= VALUE_LAG:
            ys.append(value_row(t - VALUE_LAG, acts.pop(t - VALUE_LAG)))
    for t in sorted(acts):
        ys.append(value_row(t, acts[t]))
    o_ref[...] = _rms(x_ref[...] + jnp.concatenate(ys, axis=0), gf_ref[...])

    @pl.when(i == nsteps - 1)
    def _():
        wait_slot(1 - slot)


def _stack_uv(u, v):
    n, d = u.shape
    nch = d // LANES
    uv = jnp.concatenate([u.reshape(n, nch, LANES), v.reshape(n, nch, LANES)], axis=1)
    return uv.reshape(n * 2 * nch, LANES)


def _experts(ids, x2, xn, gates, gf, table, tt):
    t, d = x2.shape
    ne = ids.shape[1]
    nsteps = t // tt
    ids = ids.reshape(t * ne)
    return pl.pallas_call(
        functools.partial(_experts_kernel, tt=tt, d=d, ne=ne, nsteps=nsteps),
        grid=(nsteps,),
        in_specs=[
            pl.BlockSpec((tt * ne,), lambda i: (i,), memory_space=pltpu.SMEM),
            pl.BlockSpec((tt * ne,), lambda i: (jnp.minimum(i + 1, nsteps - 1),), memory_space=pltpu.SMEM),
            pl.BlockSpec((tt, d), lambda i: (i, 0)),
            pl.BlockSpec((tt, d), lambda i: (i, 0)),
            pl.BlockSpec((tt, ne), lambda i: (i, 0)),
            pl.BlockSpec((1, d), lambda i: (0, 0)),
            pl.BlockSpec(memory_space=pl.ANY),
        ],
        out_specs=pl.BlockSpec((tt, d), lambda i: (i, 0)),
        out_shape=jax.ShapeDtypeStruct((t, d), F32),
        scratch_shapes=[
            pltpu.VMEM((2 * tt * ne * ROW_PITCH, LANES), F32),
            pltpu.SemaphoreType.DMA((2,)),
        ],
        compiler_params=_cparams(("arbitrary",)),
        name="experts",
    )(ids, ids, x2, xn, gates, gf, table)


def _block_diag(w):
    n, a, b = w.shape
    eye = jnp.eye(n, dtype=w.dtype)
    return (eye[:, None, :, None] * w[:, :, None, :]).reshape(n * a, n * b)


def _layer(x, mem, ln_mix_g, w_in, conv_w, conv_b, lru_wa, lru_ba, lru_wx, lru_bx, lru_lambda,
           gn_lru_g, gn_attn_g, w_out, ln_cross_g, ln_mem_g, w_cq, w_ck, w_cv, w_co, ln_ffn_g,
           peer_wq, peer_subkeys, peer_u, peer_v, ln_final_g, *, tm, ts, tt):
    b, s, d = x.shape
    t = b * s
    c = d // 2
    row = lambda a: a.reshape(1, -1)
    x2 = x.reshape(t, d)

    w_in_b = w_in.astype(BF16)
    lru2, qkv2 = _inproj(x2, row(ln_mix_g), w_in_b[:, :2 * c], w_in_b[:, 2 * c:], tm)

    y_lru = _lru(lru2.reshape(b, s, 2 * c), conv_w, row(conv_b),
                 _block_diag(lru_wa).astype(BF16), row(lru_ba),
                 _block_diag(lru_wx).astype(BF16), row(lru_bx),
                 row(lru_lambda), row(gn_lru_g), ts)
    y_attn = _moba(qkv2.reshape(b, s, 3 * c), c)

    w_out_b = w_out.astype(BF16)
    x1 = _outproj(x2, y_lru.reshape(t, c), y_attn.reshape(t, c), row(gn_attn_g),
                  w_out_b[:c], w_out_b[c:], tm)

    m = mem.shape[1]
    mk, mv = _memkv(mem.reshape(b * m, d), row(ln_mem_g), w_ck.astype(BF16), w_cv.astype(BF16), m)
    x2b = _cross(x1.reshape(b, s, d), row(ln_cross_g), w_cq.astype(BF16),
                 mk.reshape(b, m, d), mv.reshape(b, m, d), w_co.astype(BF16), tm)

    xr = x2b.reshape(t, d)
    xn, ids, gates = _route(xr, row(ln_ffn_g), peer_wq.astype(BF16),
                            peer_subkeys.reshape(PEER_HEADS * 2, PEER_KEYS, PEER_HALF), tm)
    out = _experts(ids, xr, xn, gates, row(ln_final_g), _stack_uv(peer_u, peer_v), tt)
    return out.reshape(b, s, d)


def kernel(x, mem, ln_mix_g, w_in, conv_w, conv_b, lru_wa, lru_ba, lru_wx, lru_bx, lru_lambda, gn_lru_g, gn_attn_g, w_out, ln_cross_g, ln_mem_g, w_cq, w_ck, w_cv, w_co, ln_ffn_g, peer_wq, peer_subkeys, peer_u, peer_v, ln_final_g):
    assert ln_mix_g.shape[0] == 1, "one layer"
    l = 0
    return _layer(x, mem, ln_mix_g[l], w_in[l], conv_w[l], conv_b[l], lru_wa[l], lru_ba[l],
                  lru_wx[l], lru_bx[l], lru_lambda[l], gn_lru_g[l], gn_attn_g[l], w_out[l],
                  ln_cross_g[l], ln_mem_g[l], w_cq[l], w_ck[l], w_cv[l], w_co[l], ln_ffn_g[l],
                  peer_wq[l], peer_subkeys[l], peer_u[l], peer_v[l], ln_final_g,
                  tm=256, ts=512, tt=16)
```

```python
import functools
import math

import jax
import jax.numpy as jnp
from jax import lax
from jax.experimental import pallas as pl
from jax.experimental.pallas import tpu as pltpu

F32 = jnp.float32
BF16 = jnp.bfloat16

EPS = 1e-6
NEG = -1e30

LRU_BLOCKS = 8
CONV_WIDTH = 4
LRU_C = 8.0
ATTN_HEADS = 8
ATTN_HEAD_DIM = 64
MOBA_BLOCK = 256
MOBA_TOPK = 3
CROSS_HEADS = 4
PEER_KEYS = 128
PEER_HEADS = 8
PEER_TOPK = 16
PEER_HALF = 128

LANES = 128
SUBLANES = 8
VMEM_LIMIT = 48 * 1024 * 1024

TOKEN_TILE = 256
LRU_TIME_TILE = 512
EXPERT_TOKEN_TILE = 16


def _cparams(sem):
    return pltpu.CompilerParams(dimension_semantics=sem, vmem_limit_bytes=VMEM_LIMIT)


def _rms(x, g):
    return x * lax.rsqrt(jnp.mean(x * x, axis=-1, keepdims=True) + EPS) * g


def _gelu(x):
    return 0.5 * x * (1.0 + lax.erf(x * (1.0 / math.sqrt(2.0))))


def _inproj_kernel(x_ref, g_ref, wl_ref, wq_ref, lru_ref, qkv_ref):
    h = _rms(x_ref[...], g_ref[...]).astype(BF16)
    lru_ref[...] = jnp.dot(h, wl_ref[...], preferred_element_type=F32)
    qkv_ref[...] = jnp.dot(h, wq_ref[...], preferred_element_type=F32)


def _inproj(x2, g, w_lru, w_qkv, tm):
    t, d = x2.shape
    nl, nq = w_lru.shape[1], w_qkv.shape[1]
    return pl.pallas_call(
        _inproj_kernel,
        grid=(t // tm,),
        in_specs=[
            pl.BlockSpec((tm, d), lambda i: (i, 0)),
            pl.BlockSpec((1, d), lambda i: (0, 0)),
            pl.BlockSpec((d, nl), lambda i: (0, 0)),
            pl.BlockSpec((d, nq), lambda i: (0, 0)),
        ],
        out_specs=[
            pl.BlockSpec((tm, nl), lambda i: (i, 0)),
            pl.BlockSpec((tm, nq), lambda i: (i, 0)),
        ],
        out_shape=[jax.ShapeDtypeStruct((t, nl), F32), jax.ShapeDtypeStruct((t, nq), F32)],
        compiler_params=_cparams(("parallel",)),
        name="inproj",
    )(x2, g, w_lru, w_qkv)


def _lru_kernel(lru_ref, cw_ref, cb_ref, wa_ref, ba_ref, wx_ref, bx_ref, lam_ref, gn_ref,
                y_ref, xp_ref, a_ref, b_ref, h_ref, hc_ref, *, ts, c):
    ti = pl.program_id(1)

    @pl.when(ti == 0)
    def _():
        xp_ref[0:SUBLANES, :] = jnp.zeros((SUBLANES, c), F32)
        hc_ref[...] = jnp.zeros((SUBLANES, c), F32)

    x = lru_ref[0, :, 0:c]
    gate = lru_ref[0, :, c:2 * c]
    xp_ref[SUBLANES:SUBLANES + ts, :] = x
    xc = cb_ref[...] + cw_ref[CONV_WIDTH - 1:CONV_WIDTH, :] * x
    for k in range(CONV_WIDTH - 1):
        off = SUBLANES - (CONV_WIDTH - 1) + k
        xc = xc + cw_ref[k:k + 1, :] * xp_ref[off:off + ts, :]
    xp_ref[0:SUBLANES, :] = x[ts - SUBLANES:ts, :]

    xb = xc.astype(BF16)
    gate_r = jax.nn.sigmoid(jnp.dot(xb, wa_ref[...], preferred_element_type=F32) + ba_ref[...])
    gate_i = jax.nn.sigmoid(jnp.dot(xb, wx_ref[...], preferred_element_type=F32) + bx_ref[...])
    z = -lam_ref[...]
    softplus = jnp.maximum(z, 0.0) + jnp.log1p(jnp.exp(-jnp.abs(z)))
    log_a = (-LRU_C * gate_r) * softplus
    a = jnp.exp(log_a)
    one_minus_a2 = -jnp.tanh(log_a) * (a * a + 1.0)
    a_ref[...] = a
    b_ref[...] = jnp.sqrt(one_minus_a2) * (gate_i * xc)

    row = lax.broadcasted_iota(jnp.int32, (SUBLANES, c), 0)

    def chunk(i, hprev):
        r0 = pl.multiple_of(i * SUBLANES, SUBLANES)
        a8 = a_ref[pl.ds(r0, SUBLANES), :]
        b8 = b_ref[pl.ds(r0, SUBLANES), :]
        for s in (1, 2, 4):
            keep = row >= s
            a_sh = pltpu.roll(a8, s, axis=0)
            b_sh = pltpu.roll(b8, s, axis=0)
            b8 = jnp.where(keep, b8 + a8 * b_sh, b8)
            a8 = jnp.where(keep, a8 * a_sh, a8)
        h8 = b8 + a8 * hprev
        h_ref[pl.ds(r0, SUBLANES), :] = h8
        return jnp.broadcast_to(h8[SUBLANES - 1:SUBLANES, :], (SUBLANES, c))

    hlast = lax.fori_loop(0, ts // SUBLANES, chunk, hc_ref[...], unroll=4)
    hc_ref[...] = hlast

    y = h_ref[...] * _gelu(gate)
    y_ref[0] = _rms(y, gn_ref[...])


def _lru(lru3, conv_w, conv_b, wa, ba, wx, bx, lam, gn, ts):
    b, s, c2 = lru3.shape
    c = c2 // 2
    vec = lambda: pl.BlockSpec((1, c), lambda i, j: (0, 0))
    mat = lambda: pl.BlockSpec((c, c), lambda i, j: (0, 0))
    return pl.pallas_call(
        functools.partial(_lru_kernel, ts=ts, c=c),
        grid=(b, s // ts),
        in_specs=[
            pl.BlockSpec((1, ts, c2), lambda i, j: (i, j, 0)),
            pl.BlockSpec((CONV_WIDTH, c), lambda i, j: (0, 0)),
            vec(), mat(), vec(), mat(), vec(), vec(), vec(),
        ],
        out_specs=pl.BlockSpec((1, ts, c), lambda i, j: (i, j, 0)),
        out_shape=jax.ShapeDtypeStruct((b, s, c), F32),
        scratch_shapes=[
            pltpu.VMEM((ts + SUBLANES, c), F32),
            pltpu.VMEM((ts, c), F32),
            pltpu.VMEM((ts, c), F32),
            pltpu.VMEM((ts, c), F32),
            pltpu.VMEM((SUBLANES, c), F32),
        ],
        compiler_params=_cparams(("parallel", "arbitrary")),
        name="lru",
    )(lru3, conv_w, conv_b, wa, ba, wx, bx, lam, gn)


def _moba_kernel(q_ref, k_ref, v_ref, o_ref, qb_ref, kb_ref, vb_ref, sel_ref, ex_ref, *, s):
    hh = pl.program_id(2)
    nb = s // MOBA_BLOCK
    lane = lax.broadcasted_iota(jnp.int32, (1, LANES), 1)
    hmask = (lane // ATTN_HEAD_DIM) == hh
    q = jnp.where(hmask, q_ref[0], 0.0)
    k = k_ref[0]
    kb_ref[...] = k.astype(BF16)
    vb_ref[...] = v_ref[0].astype(BF16)
    scale = ATTN_HEAD_DIM ** -0.5
    assert math.frexp(scale)[0] == 0.5, "folding the scale into q is exact only for a power of two"
    qb_ref[...] = (q * scale).astype(BF16)

    kmean = jnp.mean(k.reshape(nb, MOBA_BLOCK, LANES), axis=1)
    gate_t = lax.dot_general(kmean, q, (((1,), (1,)), ((), ())),
                             precision=lax.Precision.HIGHEST, preferred_element_type=F32)
    n_io = lax.broadcasted_iota(jnp.int32, (nb, s), 0)
    qblk = lax.broadcasted_iota(jnp.int32, (nb, s), 1) // MOBA_BLOCK
    rank = jnp.zeros((nb, s), jnp.int32)
    for m in range(nb):
        gm = gate_t[m:m + 1, :]
        beats = (gm > gate_t) | ((gm == gate_t) & (m < n_io))
        rank = rank + jnp.where(beats & (m < qblk), 1, 0)
    sel_t = jnp.where((n_io < qblk) & (rank < MOBA_TOPK), 1.0, 0.0)
    sel_t = jnp.concatenate([sel_t, jnp.zeros((LANES - nb, s), F32)], axis=0)
    sel_ref[...] = sel_t.T.astype(BF16)

    e_n = lax.broadcasted_iota(jnp.int32, (LANES, s), 0)
    e_k = lax.broadcasted_iota(jnp.int32, (LANES, s), 1) // MOBA_BLOCK
    ex_ref[...] = jnp.where(e_n == e_k, 1.0, 0.0).astype(BF16)

    causal = (lax.broadcasted_iota(jnp.int32, (MOBA_BLOCK, MOBA_BLOCK), 1)
              <= lax.broadcasted_iota(jnp.int32, (MOBA_BLOCK, MOBA_BLOCK), 0))
    for j in range(nb):
        nk = (j + 1) * MOBA_BLOCK
        r0 = j * MOBA_BLOCK
        qj = qb_ref[r0:r0 + MOBA_BLOCK, :]
        own = lax.dot_general(qj, kb_ref[r0:nk, :], (((1,), (1,)), ((), ())), preferred_element_type=F32)
        own = jnp.where(causal, own, NEG)
        if j == 0:
            sc = own
        else:
            pst = lax.dot_general(qj, kb_ref[0:r0, :], (((1,), (1,)), ((), ())), preferred_element_type=F32)
            flag = jnp.dot(sel_ref[r0:r0 + MOBA_BLOCK, :], ex_ref[:, 0:r0], preferred_element_type=F32)
            sc = jnp.concatenate([jnp.where(flag > 0.5, pst, NEG), own], axis=1)
        mx = jnp.max(sc, axis=-1, keepdims=True)
        p = jnp.exp(sc - mx)
        l = jnp.sum(p, axis=-1, keepdims=True)
        o = jnp.dot(p.astype(BF16), vb_ref[0:nk, :], preferred_element_type=F32) / l
        o = jnp.where(hmask, o, 0.0)

        @pl.when(hh == 0)
        def _():
            o_ref[0, r0:r0 + MOBA_BLOCK, :] = o

        @pl.when(hh != 0)
        def _():
            o_ref[0, r0:r0 + MOBA_BLOCK, :] += o


def _moba(qkv3, d_attn):
    b, s, _ = qkv3.shape
    npair = d_attn // LANES
    per = LANES // ATTN_HEAD_DIM
    return pl.pallas_call(
        functools.partial(_moba_kernel, s=s),
        grid=(b, npair, per),
        in_specs=[
            pl.BlockSpec((1, s, LANES), lambda i, p, h: (i, 0, p)),
            pl.BlockSpec((1, s, LANES), lambda i, p, h: (i, 0, npair + p)),
            pl.BlockSpec((1, s, LANES), lambda i, p, h: (i, 0, 2 * npair + p)),
        ],
        out_specs=pl.BlockSpec((1, s, LANES), lambda i, p, h: (i, 0, p)),
        out_shape=jax.ShapeDtypeStruct((b, s, d_attn), F32),
        scratch_shapes=[
            pltpu.VMEM((s, LANES), BF16),
            pltpu.VMEM((s, LANES), BF16),
            pltpu.VMEM((s, LANES), BF16),
            pltpu.VMEM((s, LANES), BF16),
            pltpu.VMEM((LANES, s), BF16),
        ],
        compiler_params=_cparams(("parallel", "parallel", "arbitrary")),
        name="moba",
    )(qkv3, qkv3, qkv3)


def _outproj_kernel(x_ref, yl_ref, ya_ref, ga_ref, w1_ref, w2_ref, o_ref):
    ya = _rms(ya_ref[...], ga_ref[...]).astype(BF16)
    yl = yl_ref[...].astype(BF16)
    o_ref[...] = (x_ref[...] + jnp.dot(yl, w1_ref[...], preferred_element_type=F32)
                  + jnp.dot(ya, w2_ref[...], preferred_element_type=F32))


def _outproj(x2, yl, ya, ga, w1, w2, tm):
    t, d = x2.shape
    c = yl.shape[1]
    return pl.pallas_call(
        _outproj_kernel,
        grid=(t // tm,),
        in_specs=[
            pl.BlockSpec((tm, d), lambda i: (i, 0)),
            pl.BlockSpec((tm, c), lambda i: (i, 0)),
            pl.BlockSpec((tm, c), lambda i: (i, 0)),
            pl.BlockSpec((1, c), lambda i: (0, 0)),
            pl.BlockSpec((c, d), lambda i: (0, 0)),
            pl.BlockSpec((c, d), lambda i: (0, 0)),
        ],
        out_specs=pl.BlockSpec((tm, d), lambda i: (i, 0)),
        out_shape=jax.ShapeDtypeStruct((t, d), F32),
        compiler_params=_cparams(("parallel",)),
        name="outproj",
    )(x2, yl, ya, ga, w1, w2)


def _memkv_kernel(m_ref, g_ref, wk_ref, wv_ref, k_ref, v_ref):
    mn = _rms(m_ref[...], g_ref[...]).astype(BF16)
    k_ref[...] = jnp.dot(mn, wk_ref[...], preferred_element_type=F32).astype(BF16)
    v_ref[...] = jnp.dot(mn, wv_ref[...], preferred_element_type=F32).astype(BF16)


def _memkv(mem2, g, wk, wv, tm):
    t, d = mem2.shape
    return pl.pallas_call(
        _memkv_kernel,
        grid=(t // tm,),
        in_specs=[
            pl.BlockSpec((tm, d), lambda i: (i, 0)),
            pl.BlockSpec((1, d), lambda i: (0, 0)),
            pl.BlockSpec((d, d), lambda i: (0, 0)),
            pl.BlockSpec((d, d), lambda i: (0, 0)),
        ],
        out_specs=[pl.BlockSpec((tm, d), lambda i: (i, 0)), pl.BlockSpec((tm, d), lambda i: (i, 0))],
        out_shape=[jax.ShapeDtypeStruct((t, d), BF16), jax.ShapeDtypeStruct((t, d), BF16)],
        compiler_params=_cparams(("parallel",)),
        name="memkv",
    )(mem2, g, wk, wv)


def _cross_kernel(x_ref, g_ref, wq_ref, k_ref, v_ref, wo_ref, o_ref, *, d):
    x = x_ref[0]
    q = jnp.dot(_rms(x, g_ref[...]).astype(BF16), wq_ref[...], preferred_element_type=F32)
    dh = d // CROSS_HEADS
    scale = dh ** -0.5
    k = k_ref[0]
    v = v_ref[0]
    outs = []
    for h in range(CROSS_HEADS):
        qh = q[:, h * dh:(h + 1) * dh].astype(BF16)
        sc = lax.dot_general(qh, k[:, h * dh:(h + 1) * dh], (((1,), (1,)), ((), ())),
                             preferred_element_type=F32) * scale
        mx = jnp.max(sc, axis=-1, keepdims=True)
        p = jnp.exp(sc - mx)
        l = jnp.sum(p, axis=-1, keepdims=True)
        outs.append(jnp.dot(p.astype(BF16), v[:, h * dh:(h + 1) * dh], preferred_element_type=F32) / l)
    o = jnp.concatenate(outs, axis=-1).astype(BF16)
    o_ref[0] = x + jnp.dot(o, wo_ref[...], preferred_element_type=F32)


def _cross(x3, g, wq, mk3, mv3, wo, tm):
    b, s, d = x3.shape
    m = mk3.shape[1]
    return pl.pallas_call(
        functools.partial(_cross_kernel, d=d),
        grid=(b, s // tm),
        in_specs=[
            pl.BlockSpec((1, tm, d), lambda i, j: (i, j, 0)),
            pl.BlockSpec((1, d), lambda i, j: (0, 0)),
            pl.BlockSpec((d, d), lambda i, j: (0, 0)),
            pl.BlockSpec((1, m, d), lambda i, j: (i, 0, 0)),
            pl.BlockSpec((1, m, d), lambda i, j: (i, 0, 0)),
            pl.BlockSpec((d, d), lambda i, j: (0, 0)),
        ],
        out_specs=pl.BlockSpec((1, tm, d), lambda i, j: (i, j, 0)),
        out_shape=jax.ShapeDtypeStruct((b, s, d), F32),
        compiler_params=_cparams(("parallel", "parallel")),
        name="cross",
    )(x3, g, wq, mk3, mv3, wo)


def _row_index(nrow, n):
    return lax.broadcasted_iota(jnp.int32, (nrow, n), 0).astype(F32)


def _topk_rows(sc, kk, rows):
    nrow = float(sc.shape[0])
    vals, idxs = [], []
    for _ in range(kk):
        m = jnp.max(sc, axis=0, keepdims=True)
        idx = jnp.min(jnp.where(sc == m, rows, nrow), axis=0, keepdims=True)
        vals.append(m)
        idxs.append(idx)
        sc = jnp.where(rows == idx, -jnp.inf, sc)
    return jnp.concatenate(vals, axis=0), jnp.concatenate(idxs, axis=0)


def _route_kernel(x_ref, g_ref, wq_ref, sk_ref, xn_ref, ids_ref, gates_ref, *, tm):
    xn = _rms(x_ref[...], g_ref[...])
    xn_ref[...] = xn
    pq = jnp.dot(xn.astype(BF16), wq_ref[...], preferred_element_type=F32)
    kk = PEER_TOPK
    assert kk == 2 * SUBLANES, "candidate layout below is written for 16 = 2 x 8 sublanes"
    ids_all, gates_all = [], []
    key_rows = _row_index(PEER_KEYS, tm)
    for h in range(PEER_HEADS):
        tops = []
        for p in range(2):
            c0 = (h * 2 + p) * PEER_HALF
            qhp = pq[:, c0:c0 + PEER_HALF]
            st = lax.dot_general(sk_ref[h * 2 + p], qhp, (((1,), (1,)), ((), ())),
                                 precision=lax.Precision.HIGHEST, preferred_element_type=F32)
            tops.append(_topk_rows(st, kk, key_rows))
        (s0, i0), (s1, i1) = tops
        sub = lax.broadcasted_iota(jnp.int32, (SUBLANES, tm), 0)
        ps, pi = [s0[0:1, :] + s1], [i0[0:1, :] * PEER_KEYS + i1]
        for a in range(1, SUBLANES):
            nb = kk // (a + 1)
            ps.append(jnp.where(sub < nb, s0[a:a + 1, :] + s1[0:SUBLANES, :], -jnp.inf))
            pi.append(i0[a:a + 1, :] * PEER_KEYS + i1[0:SUBLANES, :])
        ps.append(s0[SUBLANES:kk, :] + s1[0:1, :])
        pi.append(i0[SUBLANES:kk, :] * PEER_KEYS + i1[0:1, :])
        cand_s = jnp.concatenate(ps, axis=0)
        cand_i = jnp.concatenate(pi, axis=0)
        ncand = cand_s.shape[0]
        rows = _row_index(ncand, tm)
        best, ids = [], []
        for _ in range(kk):
            m = jnp.max(cand_s, axis=0, keepdims=True)
            pos = jnp.min(jnp.where(cand_s == m, rows, float(ncand)), axis=0, keepdims=True)
            hit = rows == pos
            best.append(m)
            ids.append(jnp.max(jnp.where(hit, cand_i, -1.0), axis=0, keepdims=True))
            cand_s = jnp.where(hit, -jnp.inf, cand_s)
        best = jnp.concatenate(best, axis=0)
        e = jnp.exp(best - best[0:1, :])
        gates_all.append(e / jnp.sum(e, axis=0, keepdims=True))
        ids_all.append(jnp.concatenate(ids, axis=0))
    ids_ref[...] = jnp.concatenate(ids_all, axis=0).T.astype(jnp.int32)
    gates_ref[...] = jnp.concatenate(gates_all, axis=0).T


def _route(x2, g, wq, subkeys, tm):
    t, d = x2.shape
    nq = wq.shape[1]
    ne = PEER_HEADS * PEER_TOPK
    return pl.pallas_call(
        functools.partial(_route_kernel, tm=tm),
        grid=(t // tm,),
        in_specs=[
            pl.BlockSpec((tm, d), lambda i: (i, 0)),
            pl.BlockSpec((1, d), lambda i: (0, 0)),
            pl.BlockSpec((d, nq), lambda i: (0, 0)),
            pl.BlockSpec((PEER_HEADS * 2, PEER_KEYS, PEER_HALF), lambda i: (0, 0, 0)),
        ],
        out_specs=[
            pl.BlockSpec((tm, d), lambda i: (i, 0)),
            pl.BlockSpec((tm, ne), lambda i: (i, 0)),
            pl.BlockSpec((tm, ne), lambda i: (i, 0)),
        ],
        out_shape=[
            jax.ShapeDtypeStruct((t, d), F32),
            jax.ShapeDtypeStruct((t, ne), jnp.int32),
            jax.ShapeDtypeStruct((t, ne), F32),
        ],
        compiler_params=_cparams(("parallel",)),
        name="route",
    )(x2, g, wq, subkeys)


VALUE_LAG = 2
ROW_PITCH = 20


def _experts_kernel(ids_ref, idn_ref, x_ref, xn_ref, gates_ref, gf_ref, tab_ref, o_ref,
                    buf_ref, sem_ref, *, tt, d, ne, nsteps):
    i = pl.program_id(0)
    slot = lax.rem(i, 2)
    nch = d // LANES
    nrow = 2 * nch
    per = ne // nrow
    tok_rows = ne * ROW_PITCH
    slot_rows = tt * tok_rows
    contract_last = (((1,), (1,)), ((), ()))

    def issue(idr, sl, t, e0, e1):
        for e in range(e0, e1):
            src0 = pl.multiple_of(idr[t * ne + e] * nrow, nrow)
            dst0 = pl.multiple_of(sl * slot_rows + t * tok_rows + e * ROW_PITCH, 4)
            pltpu.make_async_copy(tab_ref.at[pl.ds(src0, nrow), :], buf_ref.at[pl.ds(dst0, nrow), :],
                                  sem_ref.at[sl]).start(priority=e % 2)

    def wait_slot(sl):
        n = tt * ne * nrow
        dst0 = pl.multiple_of(sl * slot_rows, nrow)
        pltpu.make_async_copy(tab_ref.at[pl.ds(0, n), :], buf_ref.at[pl.ds(dst0, n), :],
                              sem_ref.at[sl]).wait()

    def chunk(t, r):
        base = slot * slot_rows + t * tok_rows
        return buf_ref[pl.ds(base + r, ne, stride=ROW_PITCH), :].astype(BF16)

    @pl.when(i == 0)
    def _():
        def first(t, carry):
            issue(ids_ref, 0, t, 0, ne)
            return carry
        lax.fori_loop(0, tt, first, 0)

    wait_slot(slot)

    xg = xn_ref[...]
    gg = gates_ref[...]
    todo = iter([(t, e) for t in range(tt) for e in range(ne)])

    def issue_next():
        for _ in range(per):
            t, e = next(todo)
            issue(idn_ref, 1 - slot, t, e, e + 1)

    def value_row(t, act):
        w = (gg[t:t + 1, :] * _gelu(act)).astype(BF16)
        pieces = []
        for c in range(nch):
            issue_next()
            pieces.append(jnp.dot(w, chunk(t, nch + c), preferred_element_type=F32)[0:1, :])
        return jnp.concatenate(pieces, axis=1)

    ys = []
    acts = {}
    for t in range(tt):
        act = None
        for c in range(nch):
            issue_next()
            xc = jnp.broadcast_to(xg[t:t + 1, c * LANES:(c + 1) * LANES], (SUBLANES, LANES))
            part = lax.dot_general(xc.astype(BF16), chunk(t, c), contract_last,
                                   preferred_element_type=F32)
            act = part if act is None else act + part
        acts[t] = act
        if t >= VALUE_LAG:
            ys.append(value_row(t - VALUE_LAG, acts.pop(t - VALUE_LAG)))
    for t in sorted(acts):
        ys.append(value_row(t, acts[t]))
    o_ref[...] = _rms(x_ref[...] + jnp.concatenate(ys, axis=0), gf_ref[...])

    @pl.when(i == nsteps - 1)
    def _():
        wait_slot(1 - slot)


def _stack_uv(u, v):
    n, d = u.shape
    nch = d // LANES
    uv = jnp.concatenate([u.reshape(n, nch, LANES), v.reshape(n, nch, LANES)], axis=1)
    return uv.reshape(n * 2 * nch, LANES)


def _experts(ids, x2, xn, gates, gf, table, tt):
    t, d = x2.shape
    ne = ids.shape[1]
    nsteps = t // tt
    ids = ids.reshape(t * ne)
    return pl.pallas_call(
        functools.partial(_experts_kernel, tt=tt, d=d, ne=ne, nsteps=nsteps),
        grid=(nsteps,),
        in_specs=[
            pl.BlockSpec((tt * ne,), lambda i: (i,), memory_space=pltpu.SMEM),
            pl.BlockSpec((tt * ne,), lambda i: (jnp.minimum(i + 1, nsteps - 1),), memory_space=pltpu.SMEM),
            pl.BlockSpec((tt, d), lambda i: (i, 0)),
            pl.BlockSpec((tt, d), lambda i: (i, 0)),
            pl.BlockSpec((tt, ne), lambda i: (i, 0)),
            pl.BlockSpec((1, d), lambda i: (0, 0)),
            pl.BlockSpec(memory_space=pl.ANY),
        ],
        out_specs=pl.BlockSpec((tt, d), lambda i: (i, 0)),
        out_shape=jax.ShapeDtypeStruct((t, d), F32),
        scratch_shapes=[
            pltpu.VMEM((2 * tt * ne * ROW_PITCH, LANES), F32),
            pltpu.SemaphoreType.DMA((2,)),
        ],
        compiler_params=_cparams(("arbitrary",)),
        name="experts",
    )(ids, ids, x2, xn, gates, gf, table)


def _block_diag(w):
    n, a, b = w.shape
    eye = jnp.eye(n, dtype=w.dtype)
    return (eye[:, None, :, None] * w[:, :, None, :]).reshape(n * a, n * b)


def _layer(x, mem, ln_mix_g, w_in, conv_w, conv_b, lru_wa, lru_ba, lru_wx, lru_bx, lru_lambda,
           gn_lru_g, gn_attn_g, w_out, ln_cross_g, ln_mem_g, w_cq, w_ck, w_cv, w_co, ln_ffn_g,
           peer_wq, peer_subkeys, peer_u, peer_v, ln_final_g, *, tm, ts, tt):
    b, s, d = x.shape
    t = b * s
    c = d // 2
    row = lambda a: a.reshape(1, -1)
    x2 = x.reshape(t, d)

    w_in_b = w_in.astype(BF16)
    lru2, qkv2 = _inproj(x2, row(ln_mix_g), w_in_b[:, :2 * c], w_in_b[:, 2 * c:], tm)

    y_lru = _lru(lru2.reshape(b, s, 2 * c), conv_w, row(conv_b),
                 _block_diag(lru_wa).astype(BF16), row(lru_ba),
                 _block_diag(lru_wx).astype(BF16), row(lru_bx),
                 row(lru_lambda), row(gn_lru_g), ts)
    y_attn = _moba(qkv2.reshape(b, s, 3 * c), c)

    w_out_b = w_out.astype(BF16)
    x1 = _outproj(x2, y_lru.reshape(t, c), y_attn.reshape(t, c), row(gn_attn_g),
                  w_out_b[:c], w_out_b[c:], tm)

    m = mem.shape[1]
    mk, mv = _memkv(mem.reshape(b * m, d), row(ln_mem_g), w_ck.astype(BF16), w_cv.astype(BF16), m)
    x2b = _cross(x1.reshape(b, s, d), row(ln_cross_g), w_cq.astype(BF16),
                 mk.reshape(b, m, d), mv.reshape(b, m, d), w_co.astype(BF16), tm)

    xr = x2b.reshape(t, d)
    xn, ids, gates = _route(xr, row(ln_ffn_g), peer_wq.astype(BF16),
                            peer_subkeys.reshape(PEER_HEADS * 2, PEER_KEYS, PEER_HALF), tm)
    out = _experts(ids, xr, xn, gates, row(ln_final_g), _stack_uv(peer_u, peer_v), tt)
    return out.reshape(b, s, d)


def kernel(x, mem, ln_mix_g, w_in, conv_w, conv_b, lru_wa, lru_ba, lru_wx, lru_bx, lru_lambda, gn_lru_g, gn_attn_g, w_out, ln_cross_g, ln_mem_g, w_cq, w_ck, w_cv, w_co, ln_ffn_g, peer_wq, peer_subkeys, peer_u, peer_v, ln_final_g):
    assert ln_mix_g.shape[0] == 1, "one layer"
    l = 0
    return _layer(x, mem, ln_mix_g[l], w_in[l], conv_w[l], conv_b[l], lru_wa[l], lru_ba[l],
                  lru_wx[l], lru_bx[l], lru_lambda[l], gn_lru_g[l], gn_attn_g[l], w_out[l],
                  ln_cross_g[l], ln_mem_g[l], w_cq[l], w_ck[l], w_cv[l], w_co[l], ln_ffn_g[l],
                  peer_wq[l], peer_subkeys[l], peer_u[l], peer_v[l], ln_final_g,
                  tm=TOKEN_TILE, ts=LRU_TIME_TILE, tt=EXPERT_TOKEN_TILE)
```

```python
import functools
import math

import jax
import jax.numpy as jnp
from jax import lax
from jax.experimental import pallas as pl
from jax.experimental.pallas import tpu as pltpu

F32 = jnp.float32
BF16 = jnp.bfloat16

EPS = 1e-6
NEG = -1e30

LRU_BLOCKS = 8
CONV_WIDTH = 4
LRU_C = 8.0
ATTN_HEADS = 8
ATTN_HEAD_DIM = 64
MOBA_BLOCK = 256
MOBA_TOPK = 3
CROSS_HEADS = 4
PEER_KEYS = 128
PEER_HEADS = 8
PEER_TOPK = 16
PEER_HALF = 128

LANES = 128
SUBLANES = 8
VMEM_LIMIT = 48 * 1024 * 1024

TOKEN_TILE = 256
LRU_TIME_TILE = 512
EXPERT_TOKEN_TILE = 16


def _cparams(sem):
    return pltpu.CompilerParams(dimension_semantics=sem, vmem_limit_bytes=VMEM_LIMIT)


def _rms(x, g):
    return x * lax.rsqrt(jnp.mean(x * x, axis=-1, keepdims=True) + EPS) * g


def _gelu(x):
    return 0.5 * x * (1.0 + lax.erf(x * (1.0 / math.sqrt(2.0))))


def _inproj_kernel(x_ref, g_ref, wl_ref, wq_ref, lru_ref, qkv_ref):
    h = _rms(x_ref[...], g_ref[...]).astype(BF16)
    lru_ref[...] = jnp.dot(h, wl_ref[...], preferred_element_type=F32)
    qkv_ref[...] = jnp.dot(h, wq_ref[...], preferred_element_type=F32)


def _inproj(x2, g, w_lru, w_qkv, tm):
    t, d = x2.shape
    nl, nq = w_lru.shape[1], w_qkv.shape[1]
    return pl.pallas_call(
        _inproj_kernel,
        grid=(t // tm,),
        in_specs=[
            pl.BlockSpec((tm, d), lambda i: (i, 0)),
            pl.BlockSpec((1, d), lambda i: (0, 0)),
            pl.BlockSpec((d, nl), lambda i: (0, 0)),
            pl.BlockSpec((d, nq), lambda i: (0, 0)),
        ],
        out_specs=[
            pl.BlockSpec((tm, nl), lambda i: (i, 0)),
            pl.BlockSpec((tm, nq), lambda i: (i, 0)),
        ],
        out_shape=[jax.ShapeDtypeStruct((t, nl), F32), jax.ShapeDtypeStruct((t, nq), F32)],
        compiler_params=_cparams(("parallel",)),
        name="inproj",
    )(x2, g, w_lru, w_qkv)


def _lru_kernel(lru_ref, cw_ref, cb_ref, wa_ref, ba_ref, wx_ref, bx_ref, lam_ref, gn_ref,
                y_ref, xp_ref, a_ref, b_ref, h_ref, hc_ref, *, ts, c):
    ti = pl.program_id(1)

    @pl.when(ti == 0)
    def _():
        xp_ref[0:SUBLANES, :] = jnp.zeros((SUBLANES, c), F32)
        hc_ref[...] = jnp.zeros((SUBLANES, c), F32)

    x = lru_ref[0, :, 0:c]
    gate = lru_ref[0, :, c:2 * c]
    xp_ref[SUBLANES:SUBLANES + ts, :] = x
    xc = cb_ref[...] + cw_ref[CONV_WIDTH - 1:CONV_WIDTH, :] * x
    for k in range(CONV_WIDTH - 1):
        off = SUBLANES - (CONV_WIDTH - 1) + k
        xc = xc + cw_ref[k:k + 1, :] * xp_ref[off:off + ts, :]
    xp_ref[0:SUBLANES, :] = x[ts - SUBLANES:ts, :]

    xb = xc.astype(BF16)
    gate_r = jax.nn.sigmoid(jnp.dot(xb, wa_ref[...], preferred_element_type=F32) + ba_ref[...])
    gate_i = jax.nn.sigmoid(jnp.dot(xb, wx_ref[...], preferred_element_type=F32) + bx_ref[...])
    z = -lam_ref[...]
    softplus = jnp.maximum(z, 0.0) + jnp.log1p(jnp.exp(-jnp.abs(z)))
    log_a = (-LRU_C * gate_r) * softplus
    a = jnp.exp(log_a)
    one_minus_a2 = -jnp.tanh(log_a) * (a * a + 1.0)
    a_ref[...] = a
    b_ref[...] = jnp.sqrt(one_minus_a2) * (gate_i * xc)

    row = lax.broadcasted_iota(jnp.int32, (SUBLANES, c), 0)

    def chunk(i, hprev):
        r0 = pl.multiple_of(i * SUBLANES, SUBLANES)
        a8 = a_ref[pl.ds(r0, SUBLANES), :]
        b8 = b_ref[pl.ds(r0, SUBLANES), :]
        for s in (1, 2, 4):
            keep = row >= s
            a_sh = pltpu.roll(a8, s, axis=0)
            b_sh = pltpu.roll(b8, s, axis=0)
            b8 = jnp.where(keep, b8 + a8 * b_sh, b8)
            a8 = jnp.where(keep, a8 * a_sh, a8)
        h8 = b8 + a8 * hprev
        h_ref[pl.ds(r0, SUBLANES), :] = h8
        return jnp.broadcast_to(h8[SUBLANES - 1:SUBLANES, :], (SUBLANES, c))

    hlast = lax.fori_loop(0, ts // SUBLANES, chunk, hc_ref[...], unroll=4)
    hc_ref[...] = hlast

    y = h_ref[...] * _gelu(gate)
    y_ref[0] = _rms(y, gn_ref[...])


def _lru(lru3, conv_w, conv_b, wa, ba, wx, bx, lam, gn, ts):
    b, s, c2 = lru3.shape
    c = c2 // 2
    vec = lambda: pl.BlockSpec((1, c), lambda i, j: (0, 0))
    mat = lambda: pl.BlockSpec((c, c), lambda i, j: (0, 0))
    return pl.pallas_call(
        functools.partial(_lru_kernel, ts=ts, c=c),
        grid=(b, s // ts),
        in_specs=[
            pl.BlockSpec((1, ts, c2), lambda i, j: (i, j, 0)),
            pl.BlockSpec((CONV_WIDTH, c), lambda i, j: (0, 0)),
            vec(), mat(), vec(), mat(), vec(), vec(), vec(),
        ],
        out_specs=pl.BlockSpec((1, ts, c), lambda i, j: (i, j, 0)),
        out_shape=jax.ShapeDtypeStruct((b, s, c), F32),
        scratch_shapes=[
            pltpu.VMEM((ts + SUBLANES, c), F32),
            pltpu.VMEM((ts, c), F32),
            pltpu.VMEM((ts, c), F32),
            pltpu.VMEM((ts, c), F32),
            pltpu.VMEM((SUBLANES, c), F32),
        ],
        compiler_params=_cparams(("parallel", "arbitrary")),
        name="lru",
    )(lru3, conv_w, conv_b, wa, ba, wx, bx, lam, gn)


def _moba_kernel(q_ref, k_ref, v_ref, o_ref, qb_ref, kb_ref, vb_ref, sel_ref, ex_ref, *, s):
    hh = pl.program_id(2)
    nb = s // MOBA_BLOCK
    lane = lax.broadcasted_iota(jnp.int32, (1, LANES), 1)
    hmask = (lane // ATTN_HEAD_DIM) == hh
    q = jnp.where(hmask, q_ref[0], 0.0)
    k = k_ref[0]
    kb_ref[...] = k.astype(BF16)
    vb_ref[...] = v_ref[0].astype(BF16)
    scale = ATTN_HEAD_DIM ** -0.5
    assert math.frexp(scale)[0] == 0.5, "folding the scale into q is exact only for a power of two"
    qb_ref[...] = (q * scale).astype(BF16)

    kmean = jnp.mean(k.reshape(nb, MOBA_BLOCK, LANES), axis=1)
    gate_t = lax.dot_general(kmean, q, (((1,), (1,)), ((), ())),
                             precision=lax.Precision.HIGHEST, preferred_element_type=F32)
    n_io = lax.broadcasted_iota(jnp.int32, (nb, s), 0)
    qblk = lax.broadcasted_iota(jnp.int32, (nb, s), 1) // MOBA_BLOCK
    rank = jnp.zeros((nb, s), jnp.int32)
    for m in range(nb):
        gm = gate_t[m:m + 1, :]
        beats = (gm > gate_t) | ((gm == gate_t) & (m < n_io))
        rank = rank + jnp.where(beats & (m < qblk), 1, 0)
    sel_t = jnp.where((n_io < qblk) & (rank < MOBA_TOPK), 1.0, 0.0)
    sel_t = jnp.concatenate([sel_t, jnp.zeros((LANES - nb, s), F32)], axis=0)
    sel_ref[...] = sel_t.T.astype(BF16)

    e_n = lax.broadcasted_iota(jnp.int32, (LANES, s), 0)
    e_k = lax.broadcasted_iota(jnp.int32, (LANES, s), 1) // MOBA_BLOCK
    ex_ref[...] = jnp.where(e_n == e_k, 1.0, 0.0).astype(BF16)

    causal = (lax.broadcasted_iota(jnp.int32, (MOBA_BLOCK, MOBA_BLOCK), 1)
              <= lax.broadcasted_iota(jnp.int32, (MOBA_BLOCK, MOBA_BLOCK), 0))
    for j in range(nb):
        nk = (j + 1) * MOBA_BLOCK
        r0 = j * MOBA_BLOCK
        qj = qb_ref[r0:r0 + MOBA_BLOCK, :]
        own = lax.dot_general(qj, kb_ref[r0:nk, :], (((1,), (1,)), ((), ())), preferred_element_type=F32)
        own = jnp.where(causal, own, NEG)
        if j == 0:
            sc = own
        else:
            pst = lax.dot_general(qj, kb_ref[0:r0, :], (((1,), (1,)), ((), ())), preferred_element_type=F32)
            flag = jnp.dot(sel_ref[r0:r0 + MOBA_BLOCK, :], ex_ref[:, 0:r0], preferred_element_type=F32)
            sc = jnp.concatenate([jnp.where(flag > 0.5, pst, NEG), own], axis=1)
        mx = jnp.max(sc, axis=-1, keepdims=True)
        p = jnp.exp(sc - mx)
        l = jnp.sum(p, axis=-1, keepdims=True)
        o = jnp.dot(p.astype(BF16), vb_ref[0:nk, :], preferred_element_type=F32) / l
        o = jnp.where(hmask, o, 0.0)

        @pl.when(hh == 0)
        def _():
            o_ref[0, r0:r0 + MOBA_BLOCK, :] = o

        @pl.when(hh != 0)
        def _():
            o_ref[0, r0:r0 + MOBA_BLOCK, :] += o


def _moba(qkv3, d_attn):
    b, s, _ = qkv3.shape
    npair = d_attn // LANES
    per = LANES // ATTN_HEAD_DIM
    return pl.pallas_call(
        functools.partial(_moba_kernel, s=s),
        grid=(b, npair, per),
        in_specs=[
            pl.BlockSpec((1, s, LANES), lambda i, p, h: (i, 0, p)),
            pl.BlockSpec((1, s, LANES), lambda i, p, h: (i, 0, npair + p)),
            pl.BlockSpec((1, s, LANES), lambda i, p, h: (i, 0, 2 * npair + p)),
        ],
        out_specs=pl.BlockSpec((1, s, LANES), lambda i, p, h: (i, 0, p)),
        out_shape=jax.ShapeDtypeStruct((b, s, d_attn), F32),
        scratch_shapes=[
            pltpu.VMEM((s, LANES), BF16),
            pltpu.VMEM((s, LANES), BF16),
            pltpu.VMEM((s, LANES), BF16),
            pltpu.VMEM((s, LANES), BF16),
            pltpu.VMEM((LANES, s), BF16),
        ],
        compiler_params=_cparams(("parallel", "parallel", "arbitrary")),
        name="moba",
    )(qkv3, qkv3, qkv3)


def _outproj_kernel(x_ref, yl_ref, ya_ref, ga_ref, w1_ref, w2_ref, o_ref):
    ya = _rms(ya_ref[...], ga_ref[...]).astype(BF16)
    yl = yl_ref[...].astype(BF16)
    o_ref[...] = (x_ref[...] + jnp.dot(yl, w1_ref[...], preferred_element_type=F32)
                  + jnp.dot(ya, w2_ref[...], preferred_element_type=F32))


def _outproj(x2, yl, ya, ga, w1, w2, tm):
    t, d = x2.shape
    c = yl.shape[1]
    return pl.pallas_call(
        _outproj_kernel,
        grid=(t // tm,),
        in_specs=[
            pl.BlockSpec((tm, d), lambda i: (i, 0)),
            pl.BlockSpec((tm, c), lambda i: (i, 0)),
            pl.BlockSpec((tm, c), lambda i: (i, 0)),
            pl.BlockSpec((1, c), lambda i: (0, 0)),
            pl.BlockSpec((c, d), lambda i: (0, 0)),
            pl.BlockSpec((c, d), lambda i: (0, 0)),
        ],
        out_specs=pl.BlockSpec((tm, d), lambda i: (i, 0)),
        out_shape=jax.ShapeDtypeStruct((t, d), F32),
        compiler_params=_cparams(("parallel",)),
        name="outproj",
    )(x2, yl, ya, ga, w1, w2)


def _memkv_kernel(m_ref, g_ref, wk_ref, wv_ref, k_ref, v_ref):
    mn = _rms(m_ref[...], g_ref[...]).astype(BF16)
    k_ref[...] = jnp.dot(mn, wk_ref[...], preferred_element_type=F32).astype(BF16)
    v_ref[...] = jnp.dot(mn, wv_ref[...], preferred_element_type=F32).astype(BF16)


def _memkv(mem2, g, wk, wv, tm):
    t, d = mem2.shape
    return pl.pallas_call(
        _memkv_kernel,
        grid=(t // tm,),
        in_specs=[
            pl.BlockSpec((tm, d), lambda i: (i, 0)),
            pl.BlockSpec((1, d), lambda i: (0, 0)),
            pl.BlockSpec((d, d), lambda i: (0, 0)),
            pl.BlockSpec((d, d), lambda i: (0, 0)),
        ],
        out_specs=[pl.BlockSpec((tm, d), lambda i: (i, 0)), pl.BlockSpec((tm, d), lambda i: (i, 0))],
        out_shape=[jax.ShapeDtypeStruct((t, d), BF16), jax.ShapeDtypeStruct((t, d), BF16)],
        compiler_params=_cparams(("parallel",)),
        name="memkv",
    )(mem2, g, wk, wv)


def _cross_kernel(x_ref, g_ref, wq_ref, k_ref, v_ref, wo_ref, o_ref, *, d):
    o_ref[0] = _cross_body(x_ref[0], g_ref, wq_ref, k_ref[0], v_ref[0], wo_ref, d)


def _cross_body(x, g_ref, wq_ref, k, v, wo_ref, d):
    q = jnp.dot(_rms(x, g_ref[...]).astype(BF16), wq_ref[...], preferred_element_type=F32)
    dh = d // CROSS_HEADS
    scale = dh ** -0.5
    outs = []
    for h in range(CROSS_HEADS):
        qh = q[:, h * dh:(h + 1) * dh].astype(BF16)
        sc = lax.dot_general(qh, k[:, h * dh:(h + 1) * dh], (((1,), (1,)), ((), ())),
                             preferred_element_type=F32) * scale
        mx = jnp.max(sc, axis=-1, keepdims=True)
        p = jnp.exp(sc - mx)
        l = jnp.sum(p, axis=-1, keepdims=True)
        outs.append(jnp.dot(p.astype(BF16), v[:, h * dh:(h + 1) * dh], preferred_element_type=F32) / l)
    o = jnp.concatenate(outs, axis=-1).astype(BF16)
    return x + jnp.dot(o, wo_ref[...], preferred_element_type=F32)


def _cross(x3, g, wq, mk3, mv3, wo, tm):
    b, s, d = x3.shape
    m = mk3.shape[1]
    return pl.pallas_call(
        functools.partial(_cross_kernel, d=d),
        grid=(b, s // tm),
        in_specs=[
            pl.BlockSpec((1, tm, d), lambda i, j: (i, j, 0)),
            pl.BlockSpec((1, d), lambda i, j: (0, 0)),
            pl.BlockSpec((d, d), lambda i, j: (0, 0)),
            pl.BlockSpec((1, m, d), lambda i, j: (i, 0, 0)),
            pl.BlockSpec((1, m, d), lambda i, j: (i, 0, 0)),
            pl.BlockSpec((d, d), lambda i, j: (0, 0)),
        ],
        out_specs=pl.BlockSpec((1, tm, d), lambda i, j: (i, j, 0)),
        out_shape=jax.ShapeDtypeStruct((b, s, d), F32),
        compiler_params=_cparams(("parallel", "parallel")),
        name="cross",
    )(x3, g, wq, mk3, mv3, wo)


def _row_index(nrow, n):
    return lax.broadcasted_iota(jnp.int32, (nrow, n), 0).astype(F32)


def _topk_rows(sc, kk, rows):
    nrow = float(sc.shape[0])
    vals, idxs = [], []
    for _ in range(kk):
        m = jnp.max(sc, axis=0, keepdims=True)
        idx = jnp.min(jnp.where(sc == m, rows, nrow), axis=0, keepdims=True)
        vals.append(m)
        idxs.append(idx)
        sc = jnp.where(rows == idx, -jnp.inf, sc)
    return jnp.concatenate(vals, axis=0), jnp.concatenate(idxs, axis=0)


def _route_kernel(x_ref, g_ref, wq_ref, sk_ref, xn_ref, ids_ref, gates_ref, *, tm):
    xn_ref[...], ids_ref[...], gates_ref[...] = _route_body(x_ref[...], g_ref, wq_ref, sk_ref, tm)


def _route_body(x, g_ref, wq_ref, sk_ref, tm):
    xn = _rms(x, g_ref[...])
    pq = jnp.dot(xn.astype(BF16), wq_ref[...], preferred_element_type=F32)
    kk = PEER_TOPK
    assert kk == 2 * SUBLANES, "candidate layout below is written for 16 = 2 x 8 sublanes"
    ids_all, gates_all = [], []
    key_rows = _row_index(PEER_KEYS, tm)
    for h in range(PEER_HEADS):
        tops = []
        for p in range(2):
            c0 = (h * 2 + p) * PEER_HALF
            qhp = pq[:, c0:c0 + PEER_HALF]
            st = lax.dot_general(sk_ref[h * 2 + p], qhp, (((1,), (1,)), ((), ())),
                                 precision=lax.Precision.HIGHEST, preferred_element_type=F32)
            tops.append(_topk_rows(st, kk, key_rows))
        (s0, i0), (s1, i1) = tops
        sub = lax.broadcasted_iota(jnp.int32, (SUBLANES, tm), 0)
        ps, pi = [s0[0:1, :] + s1], [i0[0:1, :] * PEER_KEYS + i1]
        for a in range(1, SUBLANES):
            nb = kk // (a + 1)
            ps.append(jnp.where(sub < nb, s0[a:a + 1, :] + s1[0:SUBLANES, :], -jnp.inf))
            pi.append(i0[a:a + 1, :] * PEER_KEYS + i1[0:SUBLANES, :])
        ps.append(s0[SUBLANES:kk, :] + s1[0:1, :])
        pi.append(i0[SUBLANES:kk, :] * PEER_KEYS + i1[0:1, :])
        cand_s = jnp.concatenate(ps, axis=0)
        cand_i = jnp.concatenate(pi, axis=0)
        ncand = cand_s.shape[0]
        rows = _row_index(ncand, tm)
        best, ids = [], []
        for _ in range(kk):
            m = jnp.max(cand_s, axis=0, keepdims=True)
            pos = jnp.min(jnp.where(cand_s == m, rows, float(ncand)), axis=0, keepdims=True)
            hit = rows == pos
            best.append(m)
            ids.append(jnp.max(jnp.where(hit, cand_i, -1.0), axis=0, keepdims=True))
            cand_s = jnp.where(hit, -jnp.inf, cand_s)
        best = jnp.concatenate(best, axis=0)
        e = jnp.exp(best - best[0:1, :])
        gates_all.append(e / jnp.sum(e, axis=0, keepdims=True))
        ids_all.append(jnp.concatenate(ids, axis=0))
    ids = jnp.concatenate(ids_all, axis=0).T.astype(jnp.int32)
    return xn, ids, jnp.concatenate(gates_all, axis=0).T


def _mid_kernel(x_ref, yl_ref, ya_ref, ga_ref, w1_ref, w2_ref, gc_ref, wq_ref, k_ref, v_ref, wo_ref,
                gr_ref, pwq_ref, sk_ref, x2_ref, xn_ref, ids_ref, gates_ref, *, tm, d):
    ya = _rms(ya_ref[0], ga_ref[...]).astype(BF16)
    x1 = (x_ref[0] + jnp.dot(yl_ref[0].astype(BF16), w1_ref[...], preferred_element_type=F32)
          + jnp.dot(ya, w2_ref[...], preferred_element_type=F32))
    x2 = _cross_body(x1, gc_ref, wq_ref, k_ref[0], v_ref[0], wo_ref, d)
    x2_ref[0] = x2
    xn_ref[0], ids_ref[0], gates_ref[0] = _route_body(x2, gr_ref, pwq_ref, sk_ref, tm)


def _mid(x3, yl3, ya3, ga, w1, w2, gc, wq, mk3, mv3, wo, gr, pwq, subkeys, tm):
    b, s, d = x3.shape
    c = yl3.shape[2]
    m = mk3.shape[1]
    nq = pwq.shape[1]
    ne = PEER_HEADS * PEER_TOPK
    tok = lambda w: pl.BlockSpec((1, tm, w), lambda i, j: (i, j, 0))
    const = lambda *shape: pl.BlockSpec(shape, lambda i, j: (0,) * len(shape))
    return pl.pallas_call(
        functools.partial(_mid_kernel, tm=tm, d=d),
        grid=(b, s // tm),
        in_specs=[
            tok(d), tok(c), tok(c), const(1, c), const(c, d), const(c, d),
            const(1, d), const(d, d),
            pl.BlockSpec((1, m, d), lambda i, j: (i, 0, 0)),
            pl.BlockSpec((1, m, d), lambda i, j: (i, 0, 0)),
            const(d, d), const(1, d), const(d, nq),
            const(PEER_HEADS * 2, PEER_KEYS, PEER_HALF),
        ],
        out_specs=[tok(d), tok(d), tok(ne), tok(ne)],
        out_shape=[
            jax.ShapeDtypeStruct((b, s, d), F32),
            jax.ShapeDtypeStruct((b, s, d), F32),
            jax.ShapeDtypeStruct((b, s, ne), jnp.int32),
            jax.ShapeDtypeStruct((b, s, ne), F32),
        ],
        compiler_params=_cparams(("parallel", "parallel")),
        name="mid",
    )(x3, yl3, ya3, ga, w1, w2, gc, wq, mk3, mv3, wo, gr, pwq, subkeys)


def _route(x2, g, wq, subkeys, tm):
    t, d = x2.shape
    nq = wq.shape[1]
    ne = PEER_HEADS * PEER_TOPK
    return pl.pallas_call(
        functools.partial(_route_kernel, tm=tm),
        grid=(t // tm,),
        in_specs=[
            pl.BlockSpec((tm, d), lambda i: (i, 0)),
            pl.BlockSpec((1, d), lambda i: (0, 0)),
            pl.BlockSpec((d, nq), lambda i: (0, 0)),
            pl.BlockSpec((PEER_HEADS * 2, PEER_KEYS, PEER_HALF), lambda i: (0, 0, 0)),
        ],
        out_specs=[
            pl.BlockSpec((tm, d), lambda i: (i, 0)),
            pl.BlockSpec((tm, ne), lambda i: (i, 0)),
            pl.BlockSpec((tm, ne), lambda i: (i, 0)),
        ],
        out_shape=[
            jax.ShapeDtypeStruct((t, d), F32),
            jax.ShapeDtypeStruct((t, ne), jnp.int32),
            jax.ShapeDtypeStruct((t, ne), F32),
        ],
        compiler_params=_cparams(("parallel",)),
        name="route",
    )(x2, g, wq, subkeys)


VALUE_LAG = 2
ROW_PITCH = 20


def _experts_kernel(ids_ref, idn_ref, x_ref, xn_ref, gates_ref, gf_ref, tab_ref, o_ref,
                    buf_ref, sem_ref, *, tt, d, ne, nsteps):
    i = pl.program_id(0)
    slot = lax.rem(i, 2)
    nch = d // LANES
    nrow = 2 * nch
    per = ne // nrow
    tok_rows = ne * ROW_PITCH
    slot_rows = tt * tok_rows
    contract_last = (((1,), (1,)), ((), ()))

    def issue(idr, sl, t, e0, e1):
        for e in range(e0, e1):
            src0 = pl.multiple_of(idr[t * ne + e] * nrow, nrow)
            dst0 = pl.multiple_of(sl * slot_rows + t * tok_rows + e * ROW_PITCH, 4)
            pltpu.make_async_copy(tab_ref.at[pl.ds(src0, nrow), :], buf_ref.at[pl.ds(dst0, nrow), :],
                                  sem_ref.at[sl]).start(priority=e % 2)

    def wait_slot(sl):
        n = tt * ne * nrow
        dst0 = pl.multiple_of(sl * slot_rows, nrow)
        pltpu.make_async_copy(tab_ref.at[pl.ds(0, n), :], buf_ref.at[pl.ds(dst0, n), :],
                              sem_ref.at[sl]).wait()

    def chunk(t, r):
        base = slot * slot_rows + t * tok_rows
        return buf_ref[pl.ds(base + r, ne, stride=ROW_PITCH), :].astype(BF16)

    @pl.when(i == 0)
    def _():
        def first(t, carry):
            issue(ids_ref, 0, t, 0, ne)
            return carry
        lax.fori_loop(0, tt, first, 0)

    wait_slot(slot)

    xg = xn_ref[...]
    gg = gates_ref[...]
    todo = iter([(t, e) for t in range(tt) for e in range(ne)])

    def issue_next():
        for _ in range(per):
            t, e = next(todo)
            issue(idn_ref, 1 - slot, t, e, e + 1)

    def value_row(t, act):
        w = (gg[t:t + 1, :] * _gelu(act)).astype(BF16)
        pieces = []
        for c in range(nch):
            issue_next()
            pieces.append(jnp.dot(w, chunk(t, nch + c), preferred_element_type=F32)[0:1, :])
        return jnp.concatenate(pieces, axis=1)

    ys = []
    acts = {}
    for t in range(tt):
        act = None
        for c in range(nch):
            issue_next()
            xc = jnp.broadcast_to(xg[t:t + 1, c * LANES:(c + 1) * LANES], (SUBLANES, LANES))
            part = lax.dot_general(xc.astype(BF16), chunk(t, c), contract_last,
                                   preferred_element_type=F32)
            act = part if act is None else act + part
        acts[t] = act
        if t >= VALUE_LAG:
            ys.append(value_row(t - VALUE_LAG, acts.pop(t - VALUE_LAG)))
    for t in sorted(acts):
        ys.append(value_row(t, acts[t]))
    o_ref[...] = _rms(x_ref[...] + jnp.concatenate(ys, axis=0), gf_ref[...])

    @pl.when(i == nsteps - 1)
    def _():
        wait_slot(1 - slot)


def _stack_uv(u, v):
    n, d = u.shape
    nch = d // LANES
    uv = jnp.concatenate([u.reshape(n, nch, LANES), v.reshape(n, nch, LANES)], axis=1)
    return uv.reshape(n * 2 * nch, LANES)


def _experts(ids, x2, xn, gates, gf, table, tt):
    t, d = x2.shape
    ne = ids.shape[1]
    nsteps = t // tt
    ids = ids.reshape(t * ne)
    return pl.pallas_call(
        functools.partial(_experts_kernel, tt=tt, d=d, ne=ne, nsteps=nsteps),
        grid=(nsteps,),
        in_specs=[
            pl.BlockSpec((tt * ne,), lambda i: (i,), memory_space=pltpu.SMEM),
            pl.BlockSpec((tt * ne,), lambda i: (jnp.minimum(i + 1, nsteps - 1),), memory_space=pltpu.SMEM),
            pl.BlockSpec((tt, d), lambda i: (i, 0)),
            pl.BlockSpec((tt, d), lambda i: (i, 0)),
            pl.BlockSpec((tt, ne), lambda i: (i, 0)),
            pl.BlockSpec((1, d), lambda i: (0, 0)),
            pl.BlockSpec(memory_space=pl.ANY),
        ],
        out_specs=pl.BlockSpec((tt, d), lambda i: (i, 0)),
        out_shape=jax.ShapeDtypeStruct((t, d), F32),
        scratch_shapes=[
            pltpu.VMEM((2 * tt * ne * ROW_PITCH, LANES), F32),
            pltpu.SemaphoreType.DMA((2,)),
        ],
        compiler_params=_cparams(("arbitrary",)),
        name="experts",
    )(ids, ids, x2, xn, gates, gf, table)


def _block_diag(w):
    n, a, b = w.shape
    eye = jnp.eye(n, dtype=w.dtype)
    return (eye[:, None, :, None] * w[:, :, None, :]).reshape(n * a, n * b)


def _layer(x, mem, ln_mix_g, w_in, conv_w, conv_b, lru_wa, lru_ba, lru_wx, lru_bx, lru_lambda,
           gn_lru_g, gn_attn_g, w_out, ln_cross_g, ln_mem_g, w_cq, w_ck, w_cv, w_co, ln_ffn_g,
           peer_wq, peer_subkeys, peer_u, peer_v, ln_final_g, *, tm, ts, tt):
    b, s, d = x.shape
    t = b * s
    c = d // 2
    row = lambda a: a.reshape(1, -1)
    x2 = x.reshape(t, d)

    w_in_b = w_in.astype(BF16)
    lru2, qkv2 = _inproj(x2, row(ln_mix_g), w_in_b[:, :2 * c], w_in_b[:, 2 * c:], tm)

    y_lru = _lru(lru2.reshape(b, s, 2 * c), conv_w, row(conv_b),
                 _block_diag(lru_wa).astype(BF16), row(lru_ba),
                 _block_diag(lru_wx).astype(BF16), row(lru_bx),
                 row(lru_lambda), row(gn_lru_g), ts)
    y_attn = _moba(qkv2.reshape(b, s, 3 * c), c)

    w_out_b = w_out.astype(BF16)
    m = mem.shape[1]
    mk, mv = _memkv(mem.reshape(b * m, d), row(ln_mem_g), w_ck.astype(BF16), w_cv.astype(BF16), m)
    x2b, xn, ids, gates = _mid(
        x, y_lru, y_attn, row(gn_attn_g), w_out_b[:c], w_out_b[c:],
        row(ln_cross_g), w_cq.astype(BF16), mk.reshape(b, m, d), mv.reshape(b, m, d), w_co.astype(BF16),
        row(ln_ffn_g), peer_wq.astype(BF16),
        peer_subkeys.reshape(PEER_HEADS * 2, PEER_KEYS, PEER_HALF), tm)
    ne = ids.shape[-1]
    out = _experts(ids.reshape(t, ne), x2b.reshape(t, d), xn.reshape(t, d), gates.reshape(t, ne),
                   row(ln_final_g), _stack_uv(peer_u, peer_v), tt)
    return out.reshape(b, s, d)


def kernel(x, mem, ln_mix_g, w_in, conv_w, conv_b, lru_wa, lru_ba, lru_wx, lru_bx, lru_lambda, gn_lru_g, gn_attn_g, w_out, ln_cross_g, ln_mem_g, w_cq, w_ck, w_cv, w_co, ln_ffn_g, peer_wq, peer_subkeys, peer_u, peer_v, ln_final_g):
    assert ln_mix_g.shape[0] == 1, "one layer"
    l = 0
    return _layer(x, mem, ln_mix_g[l], w_in[l], conv_w[l], conv_b[l], lru_wa[l], lru_ba[l],
                  lru_wx[l], lru_bx[l], lru_lambda[l], gn_lru_g[l], gn_attn_g[l], w_out[l],
                  ln_cross_g[l], ln_mem_g[l], w_cq[l], w_ck[l], w_cv[l], w_co[l], ln_ffn_g[l],
                  peer_wq[l], peer_subkeys[l], peer_u[l], peer_v[l], ln_final_g,
                  tm=TOKEN_TILE, ts=LRU_TIME_TILE, tt=EXPERT_TOKEN_TILE)
```

```python
import functools
import math

import jax
import jax.numpy as jnp
from jax import lax
from jax.experimental import pallas as pl
from jax.experimental.pallas import tpu as pltpu

F32 = jnp.float32
BF16 = jnp.bfloat16

EPS = 1e-6
NEG = -1e30

LRU_BLOCKS = 8
CONV_WIDTH = 4
LRU_C = 8.0
ATTN_HEADS = 8
ATTN_HEAD_DIM = 64
MOBA_BLOCK = 256
MOBA_TOPK = 3
CROSS_HEADS = 4
PEER_KEYS = 128
PEER_HEADS = 8
PEER_TOPK = 16
PEER_HALF = 128

LANES = 128
SUBLANES = 8
VMEM_LIMIT = 48 * 1024 * 1024

TOKEN_TILE = 256
LRU_TIME_TILE = 512
EXPERT_TOKEN_TILE = 16


def _cparams(sem):
    return pltpu.CompilerParams(dimension_semantics=sem, vmem_limit_bytes=VMEM_LIMIT)


def _rms(x, g):
    return x * lax.rsqrt(jnp.mean(x * x, axis=-1, keepdims=True) + EPS) * g


def _gelu(x):
    return 0.5 * x * (1.0 + lax.erf(x * (1.0 / math.sqrt(2.0))))


def _inproj_kernel(x_ref, g_ref, wl_ref, wq_ref, lru_ref, qkv_ref):
    h = _rms(x_ref[...], g_ref[...]).astype(BF16)
    lru_ref[...] = jnp.dot(h, wl_ref[...], preferred_element_type=F32)
    qkv_ref[...] = jnp.dot(h, wq_ref[...], preferred_element_type=F32)


def _inproj(x2, g, w_lru, w_qkv, tm):
    t, d = x2.shape
    nl, nq = w_lru.shape[1], w_qkv.shape[1]
    return pl.pallas_call(
        _inproj_kernel,
        grid=(t // tm,),
        in_specs=[
            pl.BlockSpec((tm, d), lambda i: (i, 0)),
            pl.BlockSpec((1, d), lambda i: (0, 0)),
            pl.BlockSpec((d, nl), lambda i: (0, 0)),
            pl.BlockSpec((d, nq), lambda i: (0, 0)),
        ],
        out_specs=[
            pl.BlockSpec((tm, nl), lambda i: (i, 0)),
            pl.BlockSpec((tm, nq), lambda i: (i, 0)),
        ],
        out_shape=[jax.ShapeDtypeStruct((t, nl), F32), jax.ShapeDtypeStruct((t, nq), F32)],
        compiler_params=_cparams(("parallel",)),
        name="inproj",
    )(x2, g, w_lru, w_qkv)


def _lru_kernel(lru_ref, cw_ref, cb_ref, wa_ref, ba_ref, wx_ref, bx_ref, lam_ref, gn_ref,
                y_ref, xp_ref, a_ref, b_ref, h_ref, hc_ref, *, ts, c):
    ti = pl.program_id(1)

    @pl.when(ti == 0)
    def _():
        xp_ref[0:SUBLANES, :] = jnp.zeros((SUBLANES, c), F32)
        hc_ref[...] = jnp.zeros((SUBLANES, c), F32)

    x = lru_ref[0, :, 0:c]
    gate = lru_ref[0, :, c:2 * c]
    xp_ref[SUBLANES:SUBLANES + ts, :] = x
    xc = cb_ref[...] + cw_ref[CONV_WIDTH - 1:CONV_WIDTH, :] * x
    for k in range(CONV_WIDTH - 1):
        off = SUBLANES - (CONV_WIDTH - 1) + k
        xc = xc + cw_ref[k:k + 1, :] * xp_ref[off:off + ts, :]
    xp_ref[0:SUBLANES, :] = x[ts - SUBLANES:ts, :]

    xb = xc.astype(BF16)
    gate_r = jax.nn.sigmoid(jnp.dot(xb, wa_ref[...], preferred_element_type=F32) + ba_ref[...])
    gate_i = jax.nn.sigmoid(jnp.dot(xb, wx_ref[...], preferred_element_type=F32) + bx_ref[...])
    z = -lam_ref[...]
    softplus = jnp.maximum(z, 0.0) + jnp.log1p(jnp.exp(-jnp.abs(z)))
    log_a = (-LRU_C * gate_r) * softplus
    a = jnp.exp(log_a)
    one_minus_a2 = -jnp.tanh(log_a) * (a * a + 1.0)
    a_ref[...] = a
    b_ref[...] = jnp.sqrt(one_minus_a2) * (gate_i * xc)

    row = lax.broadcasted_iota(jnp.int32, (SUBLANES, c), 0)

    def chunk(i, hprev):
        r0 = pl.multiple_of(i * SUBLANES, SUBLANES)
        a8 = a_ref[pl.ds(r0, SUBLANES), :]
        b8 = b_ref[pl.ds(r0, SUBLANES), :]
        for s in (1, 2, 4):
            keep = row >= s
            a_sh = pltpu.roll(a8, s, axis=0)
            b_sh = pltpu.roll(b8, s, axis=0)
            b8 = jnp.where(keep, b8 + a8 * b_sh, b8)
            a8 = jnp.where(keep, a8 * a_sh, a8)
        h8 = b8 + a8 * hprev
        h_ref[pl.ds(r0, SUBLANES), :] = h8
        return jnp.broadcast_to(h8[SUBLANES - 1:SUBLANES, :], (SUBLANES, c))

    hlast = lax.fori_loop(0, ts // SUBLANES, chunk, hc_ref[...], unroll=4)
    hc_ref[...] = hlast

    y = h_ref[...] * _gelu(gate)
    y_ref[0] = _rms(y, gn_ref[...])


def _lru(lru3, conv_w, conv_b, wa, ba, wx, bx, lam, gn, ts):
    b, s, c2 = lru3.shape
    c = c2 // 2
    vec = lambda: pl.BlockSpec((1, c), lambda i, j: (0, 0))
    mat = lambda: pl.BlockSpec((c, c), lambda i, j: (0, 0))
    return pl.pallas_call(
        functools.partial(_lru_kernel, ts=ts, c=c),
        grid=(b, s // ts),
        in_specs=[
            pl.BlockSpec((1, ts, c2), lambda i, j: (i, j, 0)),
            pl.BlockSpec((CONV_WIDTH, c), lambda i, j: (0, 0)),
            vec(), mat(), vec(), mat(), vec(), vec(), vec(),
        ],
        out_specs=pl.BlockSpec((1, ts, c), lambda i, j: (i, j, 0)),
        out_shape=jax.ShapeDtypeStruct((b, s, c), F32),
        scratch_shapes=[
            pltpu.VMEM((ts + SUBLANES, c), F32),
            pltpu.VMEM((ts, c), F32),
            pltpu.VMEM((ts, c), F32),
            pltpu.VMEM((ts, c), F32),
            pltpu.VMEM((SUBLANES, c), F32),
        ],
        compiler_params=_cparams(("parallel", "arbitrary")),
        name="lru",
    )(lru3, conv_w, conv_b, wa, ba, wx, bx, lam, gn)


def _moba_kernel(q_ref, k_ref, v_ref, o_ref, qb_ref, kb_ref, vb_ref, sel_ref, ex_ref, *, s):
    nb = s // MOBA_BLOCK
    per = LANES // ATTN_HEAD_DIM
    lane = lax.broadcasted_iota(jnp.int32, (1, LANES), 1)
    k = k_ref[0]
    kb_ref[...] = k.astype(BF16)
    vb_ref[...] = v_ref[0].astype(BF16)
    scale = ATTN_HEAD_DIM ** -0.5
    assert math.frexp(scale)[0] == 0.5, "folding the scale into q is exact only for a power of two"
    kmean = jnp.mean(k.reshape(nb, MOBA_BLOCK, LANES), axis=1)
    n_io = lax.broadcasted_iota(jnp.int32, (nb, s), 0)
    qblk = lax.broadcasted_iota(jnp.int32, (nb, s), 1) // MOBA_BLOCK

    e_n = lax.broadcasted_iota(jnp.int32, (LANES, s), 0)
    e_k = lax.broadcasted_iota(jnp.int32, (LANES, s), 1) // MOBA_BLOCK
    ex_ref[...] = jnp.where(e_n == e_k, 1.0, 0.0).astype(BF16)

    hmasks = []
    for hh in range(per):
        hmask = (lane // ATTN_HEAD_DIM) == hh
        hmasks.append(hmask)
        q = jnp.where(hmask, q_ref[0], 0.0)
        qb_ref[hh] = (q * scale).astype(BF16)
        gate_t = lax.dot_general(kmean, q, (((1,), (1,)), ((), ())),
                                 precision=lax.Precision.HIGHEST, preferred_element_type=F32)
        rank = jnp.zeros((nb, s), jnp.int32)
        for m in range(nb):
            gm = gate_t[m:m + 1, :]
            beats = (gm > gate_t) | ((gm == gate_t) & (m < n_io))
            rank = rank + jnp.where(beats & (m < qblk), 1, 0)
        sel_t = jnp.where((n_io < qblk) & (rank < MOBA_TOPK), 1.0, 0.0)
        sel_t = jnp.concatenate([sel_t, jnp.zeros((LANES - nb, s), F32)], axis=0)
        sel_ref[hh] = sel_t.T.astype(BF16)

    causal = (lax.broadcasted_iota(jnp.int32, (MOBA_BLOCK, MOBA_BLOCK), 1)
              <= lax.broadcasted_iota(jnp.int32, (MOBA_BLOCK, MOBA_BLOCK), 0))
    for j in range(nb):
        nk = (j + 1) * MOBA_BLOCK
        r0 = j * MOBA_BLOCK
        acc = None
        for hh in range(per):
            qj = qb_ref[hh, r0:r0 + MOBA_BLOCK, :]
            own = lax.dot_general(qj, kb_ref[r0:nk, :], (((1,), (1,)), ((), ())), preferred_element_type=F32)
            own = jnp.where(causal, own, NEG)
            if j == 0:
                sc = own
            else:
                pst = lax.dot_general(qj, kb_ref[0:r0, :], (((1,), (1,)), ((), ())),
                                      preferred_element_type=F32)
                flag = jnp.dot(sel_ref[hh, r0:r0 + MOBA_BLOCK, :], ex_ref[:, 0:r0],
                               preferred_element_type=F32)
                sc = jnp.concatenate([jnp.where(flag > 0.5, pst, NEG), own], axis=1)
            mx = jnp.max(sc, axis=-1, keepdims=True)
            p = jnp.exp(sc - mx)
            l = jnp.sum(p, axis=-1, keepdims=True)
            o = jnp.dot(p.astype(BF16), vb_ref[0:nk, :], preferred_element_type=F32) / l
            o = jnp.where(hmasks[hh], o, 0.0)
            acc = o if acc is None else acc + o
        o_ref[0, r0:r0 + MOBA_BLOCK, :] = acc


def _moba(qkv3, d_attn):
    b, s, _ = qkv3.shape
    npair = d_attn // LANES
    per = LANES // ATTN_HEAD_DIM
    return pl.pallas_call(
        functools.partial(_moba_kernel, s=s),
        grid=(b, npair),
        in_specs=[
            pl.BlockSpec((1, s, LANES), lambda i, p: (i, 0, p)),
            pl.BlockSpec((1, s, LANES), lambda i, p: (i, 0, npair + p)),
            pl.BlockSpec((1, s, LANES), lambda i, p: (i, 0, 2 * npair + p)),
        ],
        out_specs=pl.BlockSpec((1, s, LANES), lambda i, p: (i, 0, p)),
        out_shape=jax.ShapeDtypeStruct((b, s, d_attn), F32),
        scratch_shapes=[
            pltpu.VMEM((per, s, LANES), BF16),
            pltpu.VMEM((s, LANES), BF16),
            pltpu.VMEM((s, LANES), BF16),
            pltpu.VMEM((per, s, LANES), BF16),
            pltpu.VMEM((LANES, s), BF16),
        ],
        compiler_params=_cparams(("parallel", "parallel")),
        name="moba",
    )(qkv3, qkv3, qkv3)


def _outproj_kernel(x_ref, yl_ref, ya_ref, ga_ref, w1_ref, w2_ref, o_ref):
    ya = _rms(ya_ref[...], ga_ref[...]).astype(BF16)
    yl = yl_ref[...].astype(BF16)
    o_ref[...] = (x_ref[...] + jnp.dot(yl, w1_ref[...], preferred_element_type=F32)
                  + jnp.dot(ya, w2_ref[...], preferred_element_type=F32))


def _outproj(x2, yl, ya, ga, w1, w2, tm):
    t, d = x2.shape
    c = yl.shape[1]
    return pl.pallas_call(
        _outproj_kernel,
        grid=(t // tm,),
        in_specs=[
            pl.BlockSpec((tm, d), lambda i: (i, 0)),
            pl.BlockSpec((tm, c), lambda i: (i, 0)),
            pl.BlockSpec((tm, c), lambda i: (i, 0)),
            pl.BlockSpec((1, c), lambda i: (0, 0)),
            pl.BlockSpec((c, d), lambda i: (0, 0)),
            pl.BlockSpec((c, d), lambda i: (0, 0)),
        ],
        out_specs=pl.BlockSpec((tm, d), lambda i: (i, 0)),
        out_shape=jax.ShapeDtypeStruct((t, d), F32),
        compiler_params=_cparams(("parallel",)),
        name="outproj",
    )(x2, yl, ya, ga, w1, w2)


def _memkv_kernel(m_ref, g_ref, wk_ref, wv_ref, k_ref, v_ref):
    mn = _rms(m_ref[...], g_ref[...]).astype(BF16)
    k_ref[...] = jnp.dot(mn, wk_ref[...], preferred_element_type=F32).astype(BF16)
    v_ref[...] = jnp.dot(mn, wv_ref[...], preferred_element_type=F32).astype(BF16)


def _memkv(mem2, g, wk, wv, tm):
    t, d = mem2.shape
    return pl.pallas_call(
        _memkv_kernel,
        grid=(t // tm,),
        in_specs=[
            pl.BlockSpec((tm, d), lambda i: (i, 0)),
            pl.BlockSpec((1, d), lambda i: (0, 0)),
            pl.BlockSpec((d, d), lambda i: (0, 0)),
            pl.BlockSpec((d, d), lambda i: (0, 0)),
        ],
        out_specs=[pl.BlockSpec((tm, d), lambda i: (i, 0)), pl.BlockSpec((tm, d), lambda i: (i, 0))],
        out_shape=[jax.ShapeDtypeStruct((t, d), BF16), jax.ShapeDtypeStruct((t, d), BF16)],
        compiler_params=_cparams(("parallel",)),
        name="memkv",
    )(mem2, g, wk, wv)


def _cross_kernel(x_ref, g_ref, wq_ref, k_ref, v_ref, wo_ref, o_ref, *, d):
    o_ref[0] = _cross_body(x_ref[0], g_ref, wq_ref, k_ref[0], v_ref[0], wo_ref, d)


def _cross_body(x, g_ref, wq_ref, k, v, wo_ref, d):
    q = jnp.dot(_rms(x, g_ref[...]).astype(BF16), wq_ref[...], preferred_element_type=F32)
    dh = d // CROSS_HEADS
    scale = dh ** -0.5
    outs = []
    for h in range(CROSS_HEADS):
        qh = q[:, h * dh:(h + 1) * dh].astype(BF16)
        sc = lax.dot_general(qh, k[:, h * dh:(h + 1) * dh], (((1,), (1,)), ((), ())),
                             preferred_element_type=F32) * scale
        mx = jnp.max(sc, axis=-1, keepdims=True)
        p = jnp.exp(sc - mx)
        l = jnp.sum(p, axis=-1, keepdims=True)
        outs.append(jnp.dot(p.astype(BF16), v[:, h * dh:(h + 1) * dh], preferred_element_type=F32) / l)
    o = jnp.concatenate(outs, axis=-1).astype(BF16)
    return x + jnp.dot(o, wo_ref[...], preferred_element_type=F32)


def _cross(x3, g, wq, mk3, mv3, wo, tm):
    b, s, d = x3.shape
    m = mk3.shape[1]
    return pl.pallas_call(
        functools.partial(_cross_kernel, d=d),
        grid=(b, s // tm),
        in_specs=[
            pl.BlockSpec((1, tm, d), lambda i, j: (i, j, 0)),
            pl.BlockSpec((1, d), lambda i, j: (0, 0)),
            pl.BlockSpec((d, d), lambda i, j: (0, 0)),
            pl.BlockSpec((1, m, d), lambda i, j: (i, 0, 0)),
            pl.BlockSpec((1, m, d), lambda i, j: (i, 0, 0)),
            pl.BlockSpec((d, d), lambda i, j: (0, 0)),
        ],
        out_specs=pl.BlockSpec((1, tm, d), lambda i, j: (i, j, 0)),
        out_shape=jax.ShapeDtypeStruct((b, s, d), F32),
        compiler_params=_cparams(("parallel", "parallel")),
        name="cross",
    )(x3, g, wq, mk3, mv3, wo)


def _row_index(nrow, n):
    return lax.broadcasted_iota(jnp.int32, (nrow, n), 0).astype(F32)


def _topk_rows(sc, kk, rows):
    nrow = float(sc.shape[0])
    vals, idxs = [], []
    for _ in range(kk):
        m = jnp.max(sc, axis=0, keepdims=True)
        idx = jnp.min(jnp.where(sc == m, rows, nrow), axis=0, keepdims=True)
        vals.append(m)
        idxs.append(idx)
        sc = jnp.where(rows == idx, -jnp.inf, sc)
    return jnp.concatenate(vals, axis=0), jnp.concatenate(idxs, axis=0)


def _route_kernel(x_ref, g_ref, wq_ref, sk_ref, xn_ref, ids_ref, gates_ref, *, tm):
    xn_ref[...], ids_ref[...], gates_ref[...] = _route_body(x_ref[...], g_ref, wq_ref, sk_ref, tm)


def _route_body(x, g_ref, wq_ref, sk_ref, tm):
    xn = _rms(x, g_ref[...])
    pq = jnp.dot(xn.astype(BF16), wq_ref[...], preferred_element_type=F32)
    kk = PEER_TOPK
    assert kk == 2 * SUBLANES, "candidate layout below is written for 16 = 2 x 8 sublanes"
    ids_all, gates_all = [], []
    key_rows = _row_index(PEER_KEYS, tm)
    for h in range(PEER_HEADS):
        tops = []
        for p in range(2):
            c0 = (h * 2 + p) * PEER_HALF
            qhp = pq[:, c0:c0 + PEER_HALF]
            st = lax.dot_general(sk_ref[h * 2 + p], qhp, (((1,), (1,)), ((), ())),
                                 precision=lax.Precision.HIGHEST, preferred_element_type=F32)
            tops.append(_topk_rows(st, kk, key_rows))
        (s0, i0), (s1, i1) = tops
        sub = lax.broadcasted_iota(jnp.int32, (SUBLANES, tm), 0)
        ps, pi = [s0[0:1, :] + s1], [i0[0:1, :] * PEER_KEYS + i1]
        for a in range(1, SUBLANES):
            nb = kk // (a + 1)
            ps.append(jnp.where(sub < nb, s0[a:a + 1, :] + s1[0:SUBLANES, :], -jnp.inf))
            pi.append(i0[a:a + 1, :] * PEER_KEYS + i1[0:SUBLANES, :])
        ps.append(s0[SUBLANES:kk, :] + s1[0:1, :])
        pi.append(i0[SUBLANES:kk, :] * PEER_KEYS + i1[0:1, :])
        cand_s = jnp.concatenate(ps, axis=0)
        cand_i = jnp.concatenate(pi, axis=0)
        ncand = cand_s.shape[0]
        rows = _row_index(ncand, tm)
        best, ids = [], []
        for _ in range(kk):
            m = jnp.max(cand_s, axis=0, keepdims=True)
            pos = jnp.min(jnp.where(cand_s == m, rows, float(ncand)), axis=0, keepdims=True)
            hit = rows == pos
            best.append(m)
            ids.append(jnp.max(jnp.where(hit, cand_i, -1.0), axis=0, keepdims=True))
            cand_s = jnp.where(hit, -jnp.inf, cand_s)
        best = jnp.concatenate(best, axis=0)
        e = jnp.exp(best - best[0:1, :])
        gates_all.append(e / jnp.sum(e, axis=0, keepdims=True))
        ids_all.append(jnp.concatenate(ids, axis=0))
    ids = jnp.concatenate(ids_all, axis=0).T.astype(jnp.int32)
    return xn, ids, jnp.concatenate(gates_all, axis=0).T


def _mid_kernel(x_ref, yl_ref, ya_ref, ga_ref, w1_ref, w2_ref, gc_ref, wq_ref, k_ref, v_ref, wo_ref,
                gr_ref, pwq_ref, sk_ref, x2_ref, xn_ref, ids_ref, gates_ref, *, tm, d):
    ya = _rms(ya_ref[0], ga_ref[...]).astype(BF16)
    x1 = (x_ref[0] + jnp.dot(yl_ref[0].astype(BF16), w1_ref[...], preferred_element_type=F32)
          + jnp.dot(ya, w2_ref[...], preferred_element_type=F32))
    x2 = _cross_body(x1, gc_ref, wq_ref, k_ref[0], v_ref[0], wo_ref, d)
    x2_ref[0] = x2
    xn_ref[0], ids_ref[0], gates_ref[0] = _route_body(x2, gr_ref, pwq_ref, sk_ref, tm)


def _mid(x3, yl3, ya3, ga, w1, w2, gc, wq, mk3, mv3, wo, gr, pwq, subkeys, tm):
    b, s, d = x3.shape
    c = yl3.shape[2]
    m = mk3.shape[1]
    nq = pwq.shape[1]
    ne = PEER_HEADS * PEER_TOPK
    tok = lambda w: pl.BlockSpec((1, tm, w), lambda i, j: (i, j, 0))
    const = lambda *shape: pl.BlockSpec(shape, lambda i, j: (0,) * len(shape))
    return pl.pallas_call(
        functools.partial(_mid_kernel, tm=tm, d=d),
        grid=(b, s // tm),
        in_specs=[
            tok(d), tok(c), tok(c), const(1, c), const(c, d), const(c, d),
            const(1, d), const(d, d),
            pl.BlockSpec((1, m, d), lambda i, j: (i, 0, 0)),
            pl.BlockSpec((1, m, d), lambda i, j: (i, 0, 0)),
            const(d, d), const(1, d), const(d, nq),
            const(PEER_HEADS * 2, PEER_KEYS, PEER_HALF),
        ],
        out_specs=[tok(d), tok(d), tok(ne), tok(ne)],
        out_shape=[
            jax.ShapeDtypeStruct((b, s, d), F32),
            jax.ShapeDtypeStruct((b, s, d), F32),
            jax.ShapeDtypeStruct((b, s, ne), jnp.int32),
            jax.ShapeDtypeStruct((b, s, ne), F32),
        ],
        compiler_params=_cparams(("parallel", "parallel")),
        name="mid",
    )(x3, yl3, ya3, ga, w1, w2, gc, wq, mk3, mv3, wo, gr, pwq, subkeys)


def _route(x2, g, wq, subkeys, tm):
    t, d = x2.shape
    nq = wq.shape[1]
    ne = PEER_HEADS * PEER_TOPK
    return pl.pallas_call(
        functools.partial(_route_kernel, tm=tm),
        grid=(t // tm,),
        in_specs=[
            pl.BlockSpec((tm, d), lambda i: (i, 0)),
            pl.BlockSpec((1, d), lambda i: (0, 0)),
            pl.BlockSpec((d, nq), lambda i: (0, 0)),
            pl.BlockSpec((PEER_HEADS * 2, PEER_KEYS, PEER_HALF), lambda i: (0, 0, 0)),
        ],
        out_specs=[
            pl.BlockSpec((tm, d), lambda i: (i, 0)),
            pl.BlockSpec((tm, ne), lambda i: (i, 0)),
            pl.BlockSpec((tm, ne), lambda i: (i, 0)),
        ],
        out_shape=[
            jax.ShapeDtypeStruct((t, d), F32),
            jax.ShapeDtypeStruct((t, ne), jnp.int32),
            jax.ShapeDtypeStruct((t, ne), F32),
        ],
        compiler_params=_cparams(("parallel",)),
        name="route",
    )(x2, g, wq, subkeys)


VALUE_LAG = 2
ROW_PITCH = 20


def _experts_kernel(ids_ref, idn_ref, x_ref, xn_ref, gates_ref, gf_ref, tab_ref, o_ref,
                    buf_ref, sem_ref, *, tt, d, ne, nsteps):
    i = pl.program_id(0)
    slot = lax.rem(i, 2)
    nch = d // LANES
    nrow = 2 * nch
    per = ne // nrow
    tok_rows = ne * ROW_PITCH
    slot_rows = tt * tok_rows
    contract_last = (((1,), (1,)), ((), ()))

    def issue(idr, sl, t, e0, e1):
        for e in range(e0, e1):
            src0 = pl.multiple_of(idr[t * ne + e] * nrow, nrow)
            dst0 = pl.multiple_of(sl * slot_rows + t * tok_rows + e * ROW_PITCH, 4)
            pltpu.make_async_copy(tab_ref.at[pl.ds(src0, nrow), :], buf_ref.at[pl.ds(dst0, nrow), :],
                                  sem_ref.at[sl]).start(priority=e % 2)

    def wait_slot(sl):
        n = tt * ne * nrow
        dst0 = pl.multiple_of(sl * slot_rows, nrow)
        pltpu.make_async_copy(tab_ref.at[pl.ds(0, n), :], buf_ref.at[pl.ds(dst0, n), :],
                              sem_ref.at[sl]).wait()

    def chunk(t, r):
        base = slot * slot_rows + t * tok_rows
        return buf_ref[pl.ds(base + r, ne, stride=ROW_PITCH), :].astype(BF16)

    @pl.when(i == 0)
    def _():
        def first(t, carry):
            issue(ids_ref, 0, t, 0, ne)
            return carry
        lax.fori_loop(0, tt, first, 0)

    wait_slot(slot)

    xg = xn_ref[...]
    gg = gates_ref[...]
    todo = iter([(t, e) for t in range(tt) for e in range(ne)])

    def issue_next():
        for _ in range(per):
            t, e = next(todo)
            issue(idn_ref, 1 - slot, t, e, e + 1)

    def value_row(t, act):
        w = (gg[t:t + 1, :] * _gelu(act)).astype(BF16)
        pieces = []
        for c in range(nch):
            issue_next()
            pieces.append(jnp.dot(w, chunk(t, nch + c), preferred_element_type=F32)[0:1, :])
        return jnp.concatenate(pieces, axis=1)

    ys = []
    acts = {}
    for t in range(tt):
        act = None
        for c in range(nch):
            issue_next()
            xc = jnp.broadcast_to(xg[t:t + 1, c * LANES:(c + 1) * LANES], (SUBLANES, LANES))
            part = lax.dot_general(xc.astype(BF16), chunk(t, c), contract_last,
                                   preferred_element_type=F32)
            act = part if act is None else act + part
        acts[t] = act
        if t >= VALUE_LAG:
            ys.append(value_row(t - VALUE_LAG, acts.pop(t - VALUE_LAG)))
    for t in sorted(acts):
        ys.append(value_row(t, acts[t]))
    o_ref[...] = _rms(x_ref[...] + jnp.concatenate(ys, axis=0), gf_ref[...])

    @pl.when(i == nsteps - 1)
    def _():
        wait_slot(1 - slot)


def _stack_uv(u, v):
    n, d = u.shape
    nch = d // LANES
    uv = jnp.concatenate([u.reshape(n, nch, LANES), v.reshape(n, nch, LANES)], axis=1)
    return uv.reshape(n * 2 * nch, LANES)


def _experts(ids, x2, xn, gates, gf, table, tt):
    t, d = x2.shape
    ne = ids.shape[1]
    nsteps = t // tt
    ids = ids.reshape(t * ne)
    return pl.pallas_call(
        functools.partial(_experts_kernel, tt=tt, d=d, ne=ne, nsteps=nsteps),
        grid=(nsteps,),
        in_specs=[
            pl.BlockSpec((tt * ne,), lambda i: (i,), memory_space=pltpu.SMEM),
            pl.BlockSpec((tt * ne,), lambda i: (jnp.minimum(i + 1, nsteps - 1),), memory_space=pltpu.SMEM),
            pl.BlockSpec((tt, d), lambda i: (i, 0)),
            pl.BlockSpec((tt, d), lambda i: (i, 0)),
            pl.BlockSpec((tt, ne), lambda i: (i, 0)),
            pl.BlockSpec((1, d), lambda i: (0, 0)),
            pl.BlockSpec(memory_space=pl.ANY),
        ],
        out_specs=pl.BlockSpec((tt, d), lambda i: (i, 0)),
        out_shape=jax.ShapeDtypeStruct((t, d), F32),
        scratch_shapes=[
            pltpu.VMEM((2 * tt * ne * ROW_PITCH, LANES), F32),
            pltpu.SemaphoreType.DMA((2,)),
        ],
        compiler_params=_cparams(("arbitrary",)),
        name="experts",
    )(ids, ids, x2, xn, gates, gf, table)


def _block_diag(w):
    n, a, b = w.shape
    eye = jnp.eye(n, dtype=w.dtype)
    return (eye[:, None, :, None] * w[:, :, None, :]).reshape(n * a, n * b)


def _layer(x, mem, ln_mix_g, w_in, conv_w, conv_b, lru_wa, lru_ba, lru_wx, lru_bx, lru_lambda,
           gn_lru_g, gn_attn_g, w_out, ln_cross_g, ln_mem_g, w_cq, w_ck, w_cv, w_co, ln_ffn_g,
           peer_wq, peer_subkeys, peer_u, peer_v, ln_final_g, *, tm, ts, tt):
    b, s, d = x.shape
    t = b * s
    c = d // 2
    row = lambda a: a.reshape(1, -1)
    x2 = x.reshape(t, d)

    w_in_b = w_in.astype(BF16)
    lru2, qkv2 = _inproj(x2, row(ln_mix_g), w_in_b[:, :2 * c], w_in_b[:, 2 * c:], tm)

    y_lru = _lru(lru2.reshape(b, s, 2 * c), conv_w, row(conv_b),
                 _block_diag(lru_wa).astype(BF16), row(lru_ba),
                 _block_diag(lru_wx).astype(BF16), row(lru_bx),
                 row(lru_lambda), row(gn_lru_g), ts)
    y_attn = _moba(qkv2.reshape(b, s, 3 * c), c)

    w_out_b = w_out.astype(BF16)
    m = mem.shape[1]
    mk, mv = _memkv(mem.reshape(b * m, d), row(ln_mem_g), w_ck.astype(BF16), w_cv.astype(BF16), m)
    x2b, xn, ids, gates = _mid(
        x, y_lru, y_attn, row(gn_attn_g), w_out_b[:c], w_out_b[c:],
        row(ln_cross_g), w_cq.astype(BF16), mk.reshape(b, m, d), mv.reshape(b, m, d), w_co.astype(BF16),
        row(ln_ffn_g), peer_wq.astype(BF16),
        peer_subkeys.reshape(PEER_HEADS * 2, PEER_KEYS, PEER_HALF), tm)
    ne = ids.shape[-1]
    out = _experts(ids.reshape(t, ne), x2b.reshape(t, d), xn.reshape(t, d), gates.reshape(t, ne),
                   row(ln_final_g), _stack_uv(peer_u, peer_v), tt)
    return out.reshape(b, s, d)


def kernel(x, mem, ln_mix_g, w_in, conv_w, conv_b, lru_wa, lru_ba, lru_wx, lru_bx, lru_lambda, gn_lru_g, gn_attn_g, w_out, ln_cross_g, ln_mem_g, w_cq, w_ck, w_cv, w_co, ln_ffn_g, peer_wq, peer_subkeys, peer_u, peer_v, ln_final_g):
    assert ln_mix_g.shape[0] == 1, "one layer"
    l = 0
    return _layer(x, mem, ln_mix_g[l], w_in[l], conv_w[l], conv_b[l], lru_wa[l], lru_ba[l],
                  lru_wx[l], lru_bx[l], lru_lambda[l], gn_lru_g[l], gn_attn_g[l], w_out[l],
                  ln_cross_g[l], ln_mem_g[l], w_cq[l], w_ck[l], w_cv[l], w_co[l], ln_ffn_g[l],
                  peer_wq[l], peer_subkeys[l], peer_u[l], peer_v[l], ln_final_g,
                  tm=TOKEN_TILE, ts=LRU_TIME_TILE, tt=EXPERT_TOKEN_TILE)
```
